```python
import math
import jax, jax.numpy as jnp
from jax import lax
import numpy as np

D_MODEL = 1024
BATCH = 4
SEQ = 4096
DEPTH = 2
DEC_BATCH = 128
DEC_SEQ = 1
PAST_LEN = 2048
PAGE_SIZE = 128

N_MIXERS = 2
N_CONV_LAYERS = (DEPTH + 1) // 2
N_ATTN_LAYERS = DEPTH // 2
CONV_WIDTH = 31
N_HEADS = 16
HEAD_DIM = D_MODEL // N_HEADS
MOBA_BLOCK = 256
MOBA_TOPK = 3
MOBA_Q_CHUNK = 64
T5_BUCKETS = 32
T5_MAX_DIST = 128
PEER_HEADS = 8
PEER_N_KEYS = 128
PEER_N_EXPERTS = PEER_N_KEYS * PEER_N_KEYS
PEER_KEY_HALF = 64
PEER_TOPK = 16
PEER_CHUNK = 256
PLE_DIM = 256
EPS = 1e-6

kernel_name = 'moba_conformer_peer_hybrid_step'


def rms_norm(x, g):
    x32 = x.astype(jnp.float32)
    y = x32 * lax.rsqrt(jnp.mean(x32 * x32, axis=-1, keepdims=True) + EPS)
    return (y * g.astype(jnp.float32)).astype(x.dtype)


def layer_norm(x, g, b):
    x32 = x.astype(jnp.float32)
    mu = jnp.mean(x32, axis=-1, keepdims=True)
    xc = x32 - mu
    y = xc * lax.rsqrt(jnp.mean(xc * xc, axis=-1, keepdims=True) + EPS)
    return (y * g.astype(jnp.float32) + b.astype(jnp.float32)).astype(x.dtype)


def t5_bucket(dist):
    max_exact = T5_BUCKETS // 2
    df = jnp.maximum(dist, 1).astype(jnp.float32)
    large = max_exact + (jnp.log(df / max_exact) / math.log(T5_MAX_DIST / max_exact)
                         * (T5_BUCKETS - max_exact)).astype(jnp.int32)
    large = jnp.minimum(large, T5_BUCKETS - 1)
    return jnp.where(dist < max_exact, dist, large)


def conv_module(a, hist, w_in, b_in, dw_w, dw_b, ln_g, ln_b, w_out, b_out):
    z = a @ w_in + b_in
    g = z[..., :D_MODEL] * jax.nn.sigmoid(z[..., D_MODEL:])
    zc = jnp.concatenate([hist.astype(g.dtype), g], axis=1)
    y = lax.conv_general_dilated(zc, dw_w[:, None, :].astype(zc.dtype), (1,), 'VALID',
                                 dimension_numbers=('NWC', 'WIO', 'NWC'),
                                 feature_group_count=D_MODEL) + dw_b
    y = jax.nn.silu(layer_norm(y, ln_g, ln_b))
    out = y @ w_out + b_out
    return out, zc[:, -(CONV_WIDTH - 1):]


def moba_attention(q, k, v, q_pos, rel_bias):
    n, sq, h, hd = q.shape
    l = k.shape[1]
    lp = -(-l // MOBA_BLOCK) * MOBA_BLOCK
    n_blk = lp // MOBA_BLOCK
    pad = ((0, 0), (0, lp - l), (0, 0), (0, 0))
    kb = jnp.pad(k, pad).reshape(n, n_blk, MOBA_BLOCK, h, hd)
    vb = jnp.pad(v, pad).reshape(n, n_blk, MOBA_BLOCK, h, hd)
    k_mean = kb.astype(jnp.float32).mean(axis=2)
    k_sel = min(MOBA_TOPK, n_blk)
    qc = math.gcd(sq, MOBA_Q_CHUNK)
    n_chunk = sq // qc
    q_items = q.reshape(n * n_chunk, qc, h, hd)
    pos_items = jnp.broadcast_to(q_pos.reshape(1, n_chunk, qc), (n, n_chunk, qc)).reshape(n * n_chunk, qc)
    seq_items = jnp.repeat(jnp.arange(n, dtype=jnp.int32), n_chunk)
    head_idx = jnp.arange(h)
    blk_ids = jnp.arange(n_blk)
    key_off = jnp.arange(MOBA_BLOCK, dtype=jnp.int32)

    def chunk(args):
        qch, pos, s = args
        kseq, vseq, mseq = kb[s], vb[s], k_mean[s]
        qf = qch.astype(jnp.float32) * (hd ** -0.5)
        own = pos // MOBA_BLOCK
        gate = jnp.einsum('qhd,bhd->qhb', qf, mseq)
        gate = jnp.where(blk_ids[None, None, :] < own[:, None, None], gate, -jnp.inf)
        _, sel = lax.top_k(gate, k_sel)
        valid = sel < own[:, None, None]
        own_b = jnp.broadcast_to(own[:, None, None], (qc, h, 1)).astype(sel.dtype)
        blocks = jnp.concatenate([sel, own_b], axis=-1)
        ok = jnp.concatenate([valid, jnp.ones(own_b.shape, bool)], axis=-1)
        k_g = kseq[blocks, :, head_idx[None, :, None]]
        v_g = vseq[blocks, :, head_idx[None, :, None]]
        logits = jnp.einsum('qhd,qhbjd->qhbj', qf, k_g.astype(jnp.float32))
        key_pos = blocks[..., None] * MOBA_BLOCK + key_off
        dist = pos[:, None, None, None] - key_pos
        bias = rel_bias[t5_bucket(jnp.maximum(dist, 0)), head_idx[None, :, None, None]].astype(jnp.float32)
        mask = ok[..., None] & (dist >= 0)
        logits = jnp.where(mask, logits + bias, -jnp.inf)
        probs = jax.nn.softmax(logits.reshape(qc, h, -1), axis=-1).reshape(logits.shape)
        out = jnp.einsum('qhbj,qhbjd->qhd', probs, v_g.astype(jnp.float32))
        return out.astype(v.dtype)

    out = lax.map(chunk, (q_items, pos_items, seq_items))
    return out.reshape(n, sq, h, hd)


def moba_mixer(a, k_past, v_past, q_pos, w_qkv, w_o, rel_bias):
    n, s, _ = a.shape
    qkv = (a @ w_qkv).reshape(n, s, 3, N_HEADS, HEAD_DIM)
    q, k, v = qkv[:, :, 0], qkv[:, :, 1], qkv[:, :, 2]
    if k_past is None:
        kf, vf = k, v
    else:
        kf = jnp.concatenate([k_past.astype(k.dtype), k], axis=1)
        vf = jnp.concatenate([v_past.astype(v.dtype), v], axis=1)
    o = moba_attention(q, kf, vf, q_pos, rel_bias)
    return o.reshape(n, s, N_HEADS * HEAD_DIM) @ w_o, k, v


def peer_ffn(x, w_q, sub_keys, u, v):
    shp = x.shape
    xt = x.reshape(-1, shp[-1])
    t = xt.shape[0]
    q = (xt @ w_q).astype(jnp.float32).reshape(t, PEER_HEADS, 2, PEER_KEY_HALF)
    s = jnp.einsum('thcd,hcnd->thcn', q, sub_keys.astype(jnp.float32))
    s1, i1 = lax.top_k(s[:, :, 0], PEER_TOPK)
    s2, i2 = lax.top_k(s[:, :, 1], PEER_TOPK)
    cand_s = (s1[..., :, None] + s2[..., None, :]).reshape(t, PEER_HEADS, -1)
    cand_i = (i1[..., :, None] * PEER_N_KEYS + i2[..., None, :]).reshape(t, PEER_HEADS, -1)
    top_s, top_pos = lax.top_k(cand_s, PEER_TOPK)
    e_idx = jnp.take_along_axis(cand_i, top_pos, axis=-1)
    g = jax.nn.softmax(top_s, axis=-1)
    c = min(PEER_CHUNK, t)
    tp = -(-t // c) * c
    xt_p = jnp.pad(xt, ((0, tp - t), (0, 0))).reshape(tp // c, c, shp[-1])
    e_p = jnp.pad(e_idx, ((0, tp - t), (0, 0), (0, 0))).reshape(tp // c, c, PEER_HEADS, PEER_TOPK)
    g_p = jnp.pad(g, ((0, tp - t), (0, 0), (0, 0))).reshape(tp // c, c, PEER_HEADS, PEER_TOPK)

    def chunk(args):
        xc, ec, gc = args
        act = jax.nn.gelu(jnp.einsum('cd,chkd->chk', xc, u[ec]).astype(jnp.float32), approximate=False)
        w = (gc * act).astype(xc.dtype)
        return jnp.einsum('chk,chkd->cd', w, v[ec])

    out = lax.map(chunk, (xt_p, e_p, g_p)).reshape(tp, shp[-1])[:t]
    return out.reshape(shp)


def ple_add(h, p, w_proj, w_gate, g_norm):
    gate = jax.nn.sigmoid((rms_norm(h, g_norm) @ w_gate).astype(jnp.float32))
    return h + (gate * (p @ w_proj).astype(jnp.float32)).astype(h.dtype)


def setup_inputs(seed: int = 0) -> dict:
    key = jax.random.key(seed)
    ks = jax.random.split(key, 40)
    f32 = jnp.float32

    def nrm(k, shape, scale):
        return jax.random.normal(k, shape, f32) * scale

    n_pages = PAST_LEN // PAGE_SIZE
    n_used = DEC_BATCH * n_pages
    n_phys = (5 * n_used + 3) // 4
    page_table = jax.random.permutation(ks[0], n_phys)[:n_used].reshape(DEC_BATCH, n_pages).astype(jnp.int32)
    D = D_MODEL
    return {
        'x_prompt': nrm(ks[1], (BATCH, SEQ, D), 1.0),
        'x_sample': nrm(ks[2], (DEC_BATCH, DEC_SEQ, D), 1.0),
        'state_conv': nrm(ks[3], (N_CONV_LAYERS, DEC_BATCH, CONV_WIDTH - 1, D), 0.5),
        'cache_k': nrm(ks[4], (N_ATTN_LAYERS, n_phys, PAGE_SIZE, N_HEADS, HEAD_DIM), 1.0),
        'cache_v': nrm(ks[5], (N_ATTN_LAYERS, n_phys, PAGE_SIZE, N_HEADS, HEAD_DIM), 1.0),
        'page_table': page_table,
        'p_prompt': nrm(ks[6], (DEPTH, BATCH, SEQ, PLE_DIM), 1.0),
        'p_sample': nrm(ks[7], (DEPTH, DEC_BATCH, DEC_SEQ, PLE_DIM), 1.0),
        'rel_bias': nrm(ks[8], (T5_BUCKETS, N_HEADS), 0.1),
        'norm_mix_g': 1.0 + nrm(ks[9], (DEPTH, D), 0.02),
        'norm_ffn_g': 1.0 + nrm(ks[10], (DEPTH, D), 0.02),
        'norm_ple_g': 1.0 + nrm(ks[11], (DEPTH, D), 0.02),
        'norm_final_g': 1.0 + nrm(ks[12], (D,), 0.02),
        'conv_w_in': nrm(ks[13], (N_CONV_LAYERS, D, 2 * D), D ** -0.5),
        'conv_b_in': nrm(ks[14], (N_CONV_LAYERS, 2 * D), 0.02),
        'conv_dw_w': nrm(ks[15], (N_CONV_LAYERS, CONV_WIDTH, D), CONV_WIDTH ** -0.5),
        'conv_dw_b': nrm(ks[16], (N_CONV_LAYERS, D), 0.02),
        'conv_ln_g': 1.0 + nrm(ks[17], (N_CONV_LAYERS, D), 0.02),
        'conv_ln_b': nrm(ks[18], (N_CONV_LAYERS, D), 0.02),
        'conv_w_out': nrm(ks[19], (N_CONV_LAYERS, D, D), D ** -0.5),
        'conv_b_out': nrm(ks[20], (N_CONV_LAYERS, D), 0.02),
        'attn_w_qkv': nrm(ks[21], (N_ATTN_LAYERS, D, 3 * N_HEADS * HEAD_DIM), D ** -0.5),
        'attn_w_o': nrm(ks[22], (N_ATTN_LAYERS, N_HEADS * HEAD_DIM, D), (N_HEADS * HEAD_DIM) ** -0.5),
        'peer_w_q': nrm(ks[23], (DEPTH, D, PEER_HEADS * 2 * PEER_KEY_HALF), D ** -0.5),
        'peer_sub_keys': nrm(ks[24], (DEPTH, PEER_HEADS, 2, PEER_N_KEYS, PEER_KEY_HALF), PEER_KEY_HALF ** -0.5),
        'peer_u': nrm(ks[25], (DEPTH, PEER_N_EXPERTS, D), D ** -0.5),
        'peer_v': nrm(ks[26], (DEPTH, PEER_N_EXPERTS, D), (PEER_HEADS * PEER_TOPK) ** -0.5),
        'ple_w_proj': nrm(ks[27], (DEPTH, PLE_DIM, D), PLE_DIM ** -0.5),
        'ple_w_gate': nrm(ks[28], (DEPTH, D, D), D ** -0.5),
    }


def reference(x_prompt, x_sample, state_conv, cache_k, cache_v, page_table, p_prompt, p_sample,
              rel_bias, norm_mix_g, norm_ffn_g, norm_ple_g, norm_final_g,
              conv_w_in, conv_b_in, conv_dw_w, conv_dw_b, conv_ln_g, conv_ln_b, conv_w_out, conv_b_out,
              attn_w_qkv, attn_w_o, peer_w_q, peer_sub_keys, peer_u, peer_v, ple_w_proj, ple_w_gate):
    n_b, seq = x_prompt.shape[0], x_prompt.shape[1]
    n_d, dec_seq = x_sample.shape[0], x_sample.shape[1]
    n_pages = page_table.shape[1]
    past_len = n_pages * PAGE_SIZE
    pos_p = jnp.arange(seq, dtype=jnp.int32)
    pos_s = past_len + jnp.arange(dec_seq, dtype=jnp.int32)
    hp, hs = x_prompt, x_sample
    conv_p, conv_s, kp, vp, ks_, vs_ = [], [], [], [], [], []
    for i in range(DEPTH):
        ap = rms_norm(hp, norm_mix_g[i])
        asm = rms_norm(hs, norm_mix_g[i])
        j = i // N_MIXERS
        if i % N_MIXERS == 0:
            cw = (conv_w_in[j], conv_b_in[j], conv_dw_w[j], conv_dw_b[j], conv_ln_g[j], conv_ln_b[j],
                  conv_w_out[j], conv_b_out[j])
            hist0 = jnp.zeros((n_b, CONV_WIDTH - 1, D_MODEL), hp.dtype)
            mp, st_p = conv_module(ap, hist0, *cw)
            ms, st_s = conv_module(asm, state_conv[j], *cw)
            conv_p.append(st_p)
            conv_s.append(st_s)
        else:
            k_past = cache_k[j][page_table].reshape(n_d, past_len, N_HEADS, HEAD_DIM)
            v_past = cache_v[j][page_table].reshape(n_d, past_len, N_HEADS, HEAD_DIM)
            mp, k_new_p, v_new_p = moba_mixer(ap, None, None, pos_p, attn_w_qkv[j], attn_w_o[j], rel_bias)
            ms, k_new_s, v_new_s = moba_mixer(asm, k_past, v_past, pos_s, attn_w_qkv[j], attn_w_o[j], rel_bias)
            kp.append(k_new_p)
            vp.append(v_new_p)
            ks_.append(k_new_s)
            vs_.append(v_new_s)
        hp = hp + mp
        hs = hs + ms
        hp = hp + peer_ffn(rms_norm(hp, norm_ffn_g[i]), peer_w_q[i], peer_sub_keys[i], peer_u[i], peer_v[i])
        hs = hs + peer_ffn(rms_norm(hs, norm_ffn_g[i]), peer_w_q[i], peer_sub_keys[i], peer_u[i], peer_v[i])
        hp = ple_add(hp, p_prompt[i], ple_w_proj[i], ple_w_gate[i], norm_ple_g[i])
        hs = ple_add(hs, p_sample[i], ple_w_proj[i], ple_w_gate[i], norm_ple_g[i])
    y_prompt = rms_norm(hp, norm_final_g)
    y_sample = rms_norm(hs, norm_final_g)
    new_state_conv_prompt = jnp.stack(conv_p)
    new_state_conv_sample = jnp.stack(conv_s)
    new_cache_k_prompt = jnp.stack(kp)
    new_cache_v_prompt = jnp.stack(vp)
    new_cache_k_sample = jnp.stack(ks_)
    new_cache_v_sample = jnp.stack(vs_)
    return (y_prompt, y_sample, new_state_conv_prompt, new_state_conv_sample,
            new_cache_k_prompt, new_cache_v_prompt, new_cache_k_sample, new_cache_v_sample)
```

```python
import functools
import math

import numpy as np
import jax
import jax.numpy as jnp
from jax import lax
from jax.experimental import pallas as pl
from jax.experimental.pallas import tpu as pltpu

F32 = jnp.float32
BF16 = jnp.bfloat16
EPS = 1e-6
NEG_INF = float("-inf")

D_MODEL = 1024
N_HEADS = 16
HEAD_DIM = 64
CONV_WIDTH = 31
MOBA_BLOCK = 256
MOBA_TOPK = 3
T5_BUCKETS = 32
T5_MAX_DIST = 128
PEER_HEADS = 8
PEER_N_KEYS = 128
PEER_KEY_HALF = 64
PEER_TOPK = 16
PAGE_SIZE = 128

TOKEN_TILE = 512
ROUTE_TILE = 256
EXPERT_TILE = 1024
CONV_CHUNK = 64
HALO = 32
STAT_LANES = 128
VMEM_LIMIT = 56 * 1024 * 1024

CAND_COUNT = [PEER_TOPK // (j + 1) for j in range(PEER_TOPK)]
CAND_ROWS = [16] + [8] * (PEER_TOPK - 1)
CAND_TOTAL = sum(CAND_ROWS)


def _params(*sem):
    return pltpu.CompilerParams(dimension_semantics=sem, vmem_limit_bytes=VMEM_LIMIT)


def _dot(a, b):
    return jnp.dot(a, b, preferred_element_type=F32)


def _dot_nt(a, b):
    return lax.dot_general(a, b, (((1,), (1,)), ((), ())), preferred_element_type=F32)


def _split(x):
    hi = x.astype(BF16)
    lo = (x - hi.astype(F32)).astype(BF16)
    return hi, lo


def _dot3(a_hi, a_lo, b_hi, b_lo):
    return _dot(a_hi, b_hi) + _dot(a_hi, b_lo) + _dot(a_lo, b_hi)


def _rms(x, g):
    return x * lax.rsqrt(jnp.mean(x * x, axis=-1, keepdims=True) + EPS) * g


def _sigmoid(x):
    return 1.0 / (1.0 + jnp.exp(-x))


def _row_spec(tile, width):
    return pl.BlockSpec((tile, width), lambda i: (i, 0))


def _full_spec(shape):
    nd = len(shape)
    return pl.BlockSpec(shape, lambda *_: (0,) * nd)


def _glu_in_kernel(h_ref, g_ref, w_ref, b_ref, o_ref):
    a = _rms(h_ref[...], g_ref[...]).astype(BF16)
    z = _dot(a, w_ref[...]) + b_ref[...]
    o_ref[...] = z[:, :D_MODEL] * _sigmoid(z[:, D_MODEL:])


def glu_in(h, g, w_bf, b):
    tp = h.shape[0]
    return pl.pallas_call(
        _glu_in_kernel,
        grid=(tp // TOKEN_TILE,),
        in_specs=[_row_spec(TOKEN_TILE, D_MODEL), _full_spec((1, D_MODEL)),
                  _full_spec((D_MODEL, 2 * D_MODEL)), _full_spec((1, 2 * D_MODEL))],
        out_specs=_row_spec(TOKEN_TILE, D_MODEL),
        out_shape=jax.ShapeDtypeStruct((tp, D_MODEL), F32),
        compiler_params=_params("parallel"),
        name="glu_in",
    )(h, g, w_bf, b)


def _proj_residual_kernel(h_ref, y_ref, w_ref, b_ref, o_ref):
    o_ref[...] = h_ref[...] + _dot(y_ref[...].astype(BF16), w_ref[...]) + b_ref[...]


def proj_residual(h, y, w_bf, b):
    tp = h.shape[0]
    return pl.pallas_call(
        _proj_residual_kernel,
        grid=(tp // TOKEN_TILE,),
        in_specs=[_row_spec(TOKEN_TILE, D_MODEL), _row_spec(TOKEN_TILE, D_MODEL),
                  _full_spec((D_MODEL, D_MODEL)), _full_spec((1, D_MODEL))],
        out_specs=_row_spec(TOKEN_TILE, D_MODEL),
        out_shape=jax.ShapeDtypeStruct((tp, D_MODEL), F32),
        compiler_params=_params("parallel"),
        name="proj_residual",
    )(h, y, w_bf, b)


def _ple_kernel(h_ref, p_ref, g_ref, wg_ref, wp_ref, gf_ref, o_ref, *, final_norm):
    h = h_ref[...]
    gate = _sigmoid(_dot(_rms(h, g_ref[...]).astype(BF16), wg_ref[...]))
    out = h + gate * _dot(p_ref[...].astype(BF16), wp_ref[...])
    if final_norm:
        out = _rms(out, gf_ref[...])
    o_ref[...] = out


def ple(h, p, g, wg_bf, wp_bf, g_final, final_norm):
    tp = h.shape[0]
    pdim = p.shape[1]
    return pl.pallas_call(
        functools.partial(_ple_kernel, final_norm=final_norm),
        grid=(tp // TOKEN_TILE,),
        in_specs=[_row_spec(TOKEN_TILE, D_MODEL), _row_spec(TOKEN_TILE, pdim),
                  _full_spec((1, D_MODEL)), _full_spec((D_MODEL, D_MODEL)),
                  _full_spec((pdim, D_MODEL)), _full_spec((1, D_MODEL))],
        out_specs=_row_spec(TOKEN_TILE, D_MODEL),
        out_shape=jax.ShapeDtypeStruct((tp, D_MODEL), F32),
        compiler_params=_params("parallel"),
        name="ple_final" if final_norm else "ple",
    )(h, p, g, wg_bf, wp_bf, g_final)


def _qkv_kernel(h_ref, g_ref, wq_hi_ref, wq_lo_ref, wkv_ref, q_ref, k_ref, v_ref):
    a_hi, a_lo = _split(_rms(h_ref[...], g_ref[...]))
    q_ref[...] = _dot3(a_hi, a_lo, wq_hi_ref[...], wq_lo_ref[...])
    kv = _dot(a_hi, wkv_ref[...])
    k_ref[...] = kv[:, :D_MODEL]
    v_ref[...] = kv[:, D_MODEL:]


def qkv_proj(h, g, wq_hi, wq_lo, wkv_bf):
    tp = h.shape[0]
    out = jax.ShapeDtypeStruct((tp, D_MODEL), F32)
    return pl.pallas_call(
        _qkv_kernel,
        grid=(tp // TOKEN_TILE,),
        in_specs=[_row_spec(TOKEN_TILE, D_MODEL), _full_spec((1, D_MODEL)),
                  _full_spec((D_MODEL, D_MODEL)), _full_spec((D_MODEL, D_MODEL)),
                  _full_spec((D_MODEL, 2 * D_MODEL))],
        out_specs=[_row_spec(TOKEN_TILE, D_MODEL)] * 3,
        out_shape=[out, out, out],
        compiler_params=_params("parallel"),
        name="qkv_proj",
    )(h, g, wq_hi, wq_lo, wkv_bf)


def _ln_silu(y, g, b):
    mu = jnp.mean(y, axis=-1, keepdims=True)
    yc = y - mu
    y = yc * lax.rsqrt(jnp.mean(yc * yc, axis=-1, keepdims=True) + EPS) * g + b
    return y * _sigmoid(y)


def _conv_prompt_kernel(x_ref, dwk_ref, dwb_ref, lng_ref, lnb_ref, y_ref, st_ref, buf_ref):
    tile = x_ref.shape[0]

    @pl.when(pl.program_id(1) == 0)
    def _():
        buf_ref[0:HALO, :] = jnp.zeros((HALO, D_MODEL), F32)

    buf_ref[HALO:HALO + tile, :] = x_ref[...]

    def chunk(c, carry):
        t0 = pl.multiple_of(c * CONV_CHUNK, CONV_CHUNK)
        win = buf_ref[pl.ds(t0, CONV_CHUNK + HALO), :]
        acc = None
        for s in range(8):
            part = None
            for u in range(4):
                k = 8 * u + s
                term = win[24 - 8 * u:24 - 8 * u + CONV_CHUNK + 8, :] * dwk_ref[k:k + 1, :]
                part = term if part is None else part + term
            if s:
                part = pltpu.roll(part, s, 0)
            part = part[8:8 + CONV_CHUNK, :]
            acc = part if acc is None else acc + part
        y = _ln_silu(acc + dwb_ref[...], lng_ref[...], lnb_ref[...])
        y_ref[pl.ds(t0, CONV_CHUNK), :] = y.astype(y_ref.dtype)
        return carry

    lax.fori_loop(0, tile // CONV_CHUNK, chunk, 0)
    tail = buf_ref[tile:tile + HALO, :]
    st_ref[0] = tail
    buf_ref[0:HALO, :] = tail


def conv_prompt(glu, n_b, seq, dwk, dwb, lng, lnb):
    nt = seq // TOKEN_TILE
    return pl.pallas_call(
        _conv_prompt_kernel,
        grid=(n_b, nt),
        in_specs=[pl.BlockSpec((TOKEN_TILE, D_MODEL), lambda n, t: (n * nt + t, 0)),
                  _full_spec((32, D_MODEL)), _full_spec((1, D_MODEL)),
                  _full_spec((1, D_MODEL)), _full_spec((1, D_MODEL))],
        out_specs=[pl.BlockSpec((TOKEN_TILE, D_MODEL), lambda n, t: (n * nt + t, 0)),
                   pl.BlockSpec((1, HALO, D_MODEL), lambda n, t: (n, 0, 0))],
        out_shape=[jax.ShapeDtypeStruct((n_b * seq, D_MODEL), BF16),
                   jax.ShapeDtypeStruct((n_b, HALO, D_MODEL), F32)],
        scratch_shapes=[pltpu.VMEM((TOKEN_TILE + HALO, D_MODEL), F32)],
        compiler_params=_params("arbitrary", "arbitrary"),
        name="conv_prompt",
    )(glu, dwk, dwb, lng, lnb)


def _conv_sample_kernel(x_ref, st_ref, dw_ref, dwb_ref, lng_ref, lnb_ref, y_ref):
    hist = jnp.sum(st_ref[...] * dw_ref[0:CONV_WIDTH - 1, :][None], axis=1)
    y = hist + x_ref[...] * dw_ref[CONV_WIDTH - 1:CONV_WIDTH, :] + dwb_ref[...]
    y_ref[...] = _ln_silu(y, lng_ref[...], lnb_ref[...]).astype(y_ref.dtype)


def conv_sample(glu_s, state, dw, dwb, lng, lnb):
    n_d = glu_s.shape[0]
    blk = 32 if n_d % 32 == 0 else n_d
    return pl.pallas_call(
        _conv_sample_kernel,
        grid=(n_d // blk,),
        in_specs=[_row_spec(blk, D_MODEL),
                  pl.BlockSpec((blk, CONV_WIDTH - 1, D_MODEL), lambda i: (i, 0, 0)),
                  _full_spec((CONV_WIDTH, D_MODEL)), _full_spec((1, D_MODEL)),
                  _full_spec((1, D_MODEL)), _full_spec((1, D_MODEL))],
        out_specs=_row_spec(blk, D_MODEL),
        out_shape=jax.ShapeDtypeStruct((n_d, D_MODEL), BF16),
        compiler_params=_params("parallel"),
        name="conv_sample",
    )(glu_s, state, dw, dwb, lng, lnb)


def _extract_top(w, tie_key, vals_ref, track):
    init = jnp.full(w.shape, float(PEER_TOPK) if track == "rank" else 0.0, F32)

    def body(r, carry):
        w, out = carry
        m = jnp.max(w, axis=0, keepdims=True)
        first = jnp.min(jnp.where(w == m, tie_key, 1e9), axis=0, keepdims=True)
        hit = tie_key == first
        if vals_ref is not None:
            vals_ref[pl.ds(r, 1), :] = m
        mark = r.astype(F32) if track == "rank" else 1.0
        return jnp.where(hit, NEG_INF, w), jnp.where(hit, mark, out)

    return lax.fori_loop(0, PEER_TOPK, body, (w, init))[1]


def _peer_route_kernel(x_ref, g_ref, wq_hi_ref, wq_lo_ref, keys_ref, cpos_ref, cneg_ref,
                       xt_ref, rank2_ref, e2_ref, nsel_ref, c1_ref,
                       q_scr, s_scr, v_scr):
    tile = x_ref.shape[0]
    xn_t = _rms(x_ref[...], g_ref[...]).T
    x_hi, x_lo = _split(xn_t)
    xt_ref[...] = x_hi
    q_scr[...] = (_dot(wq_hi_ref[...], x_hi) + _dot(wq_hi_ref[...], x_lo)
                  + _dot(wq_lo_ref[...], x_hi))
    key_iota = lax.broadcasted_iota(jnp.int32, (PEER_N_KEYS, tile), 0).astype(F32)
    cpos = cpos_ref[...]

    def head(h, carry):
        ranks = []
        for c in range(2):
            i = 2 * h + c
            qs = q_scr[pl.ds(pl.multiple_of(i * PEER_KEY_HALF, PEER_KEY_HALF), PEER_KEY_HALF), :]
            s = jnp.dot(keys_ref[i], qs, precision=lax.Precision.HIGHEST,
                        preferred_element_type=F32)
            s_scr[c] = s
            ranks.append(_extract_top(s, key_iota, v_scr.at[c], "rank"))
        v2_16 = v_scr[1]
        v2_8 = v_scr[1, 0:8, :]
        cand = jnp.concatenate(
            [v_scr[0, j:j + 1, :] + (v2_16 if j == 0 else v2_8) for j in range(PEER_TOPK)],
            axis=0) + cneg_ref[...]
        chosen = _extract_top(cand, cpos, None, "mask")
        max1 = v_scr[0, 0:1, :]
        max2 = v_scr[1, 0:1, :]
        z = jnp.sum(jnp.where(chosen > 0.0, jnp.exp(cand - (max1 + max2)), 0.0),
                    axis=0, keepdims=True)
        nsel = jnp.zeros((PEER_N_KEYS, tile), F32)
        row = 0
        for j in range(PEER_TOPK):
            n_j = jnp.sum(chosen[row:row + CAND_ROWS[j], :], axis=0, keepdims=True)
            nsel = jnp.where(ranks[0] == float(j), n_j, nsel)
            row += CAND_ROWS[j]
        rank2_ref[h] = ranks[1]
        nsel_ref[h] = nsel
        e2_ref[h] = jnp.exp(s_scr[1] - max2)
        c1_ref[h] = jnp.exp(s_scr[0] - max1) / z
        return carry

    lax.fori_loop(0, PEER_HEADS, head, 0)


def _cand_constants(tile):
    pos, neg = [], []
    for j in range(PEER_TOPK):
        for l in range(CAND_ROWS[j]):
            pos.append(16.0 * j + l)
            neg.append(0.0 if l < CAND_COUNT[j] else NEG_INF)
    pos = np.broadcast_to(np.asarray(pos, np.float32)[:, None], (CAND_TOTAL, tile))
    neg = np.broadcast_to(np.asarray(neg, np.float32)[:, None], (CAND_TOTAL, tile))
    return jnp.asarray(pos), jnp.asarray(neg)


def peer_route(h, g, wq_t_hi, wq_t_lo, keys):
    tp = h.shape[0]
    cpos, cneg = _cand_constants(ROUTE_TILE)
    dense = jax.ShapeDtypeStruct((PEER_HEADS, PEER_N_KEYS, tp), F32)
    dense_spec = pl.BlockSpec((PEER_HEADS, PEER_N_KEYS, ROUTE_TILE), lambda i: (0, 0, i))
    return pl.pallas_call(
        _peer_route_kernel,
        grid=(tp // ROUTE_TILE,),
        in_specs=[_row_spec(ROUTE_TILE, D_MODEL), _full_spec((1, D_MODEL)),
                  _full_spec((D_MODEL, D_MODEL)), _full_spec((D_MODEL, D_MODEL)),
                  _full_spec((2 * PEER_HEADS, PEER_N_KEYS, PEER_KEY_HALF)),
                  _full_spec((CAND_TOTAL, ROUTE_TILE)), _full_spec((CAND_TOTAL, ROUTE_TILE))],
        out_specs=[pl.BlockSpec((D_MODEL, ROUTE_TILE), lambda i: (0, i)),
                   dense_spec, dense_spec, dense_spec, dense_spec],
        out_shape=[jax.ShapeDtypeStruct((D_MODEL, tp), BF16), dense, dense, dense, dense],
        scratch_shapes=[pltpu.VMEM((D_MODEL, ROUTE_TILE), F32),
                        pltpu.VMEM((2, PEER_N_KEYS, ROUTE_TILE), F32),
                        pltpu.VMEM((2, PEER_TOPK, ROUTE_TILE), F32)],
        compiler_params=_params("parallel"),
        name="peer_route",
    )(h, g, wq_t_hi, wq_t_lo, keys, cpos, cneg)


def _peer_dense_kernel(xt_ref, u_ref, vt_ref, rank2_ref, e2_ref, nsel_ref, c1_ref, h_ref,
                       o_ref, act_scr, w_scr, acc_scr):
    j = pl.program_id(1)

    @pl.when(j == 0)
    def _():
        acc_scr[...] = jnp.zeros(acc_scr.shape, F32)

    act_scr[...] = _dot(u_ref[...], xt_ref[...])
    for a_local in range(EXPERT_TILE // PEER_N_KEYS):
        a = j * (EXPERT_TILE // PEER_N_KEYS) + a_local
        gate = None
        for h in range(PEER_HEADS):
            n_row = nsel_ref[h, pl.ds(a, 1), :]
            c_row = c1_ref[h, pl.ds(a, 1), :]
            term = jnp.where(rank2_ref[h] < n_row, e2_ref[h] * c_row, 0.0)
            gate = term if gate is None else gate + term
        rows = slice(a_local * PEER_N_KEYS, (a_local + 1) * PEER_N_KEYS)
        act = act_scr[rows, :]
        gelu = 0.5 * act * (1.0 + lax.erf(act * (1.0 / math.sqrt(2.0))))
        w_scr[rows, :] = (gate * gelu).astype(BF16)
    acc_scr[...] += _dot(vt_ref[...], w_scr[...])

    @pl.when(j == pl.num_programs(1) - 1)
    def _():
        o_ref[...] = h_ref[...] + acc_scr[...].T


def peer_dense(h, xt, u_bf, vt_bf, rank2, e2, nsel, c1):
    tp = h.shape[0]
    n_exp = u_bf.shape[0]
    dense_spec = pl.BlockSpec((PEER_HEADS, PEER_N_KEYS, TOKEN_TILE), lambda i, j: (0, 0, i))
    return pl.pallas_call(
        _peer_dense_kernel,
        grid=(tp // TOKEN_TILE, n_exp // EXPERT_TILE),
        in_specs=[pl.BlockSpec((D_MODEL, TOKEN_TILE), lambda i, j: (0, i)),
                  pl.BlockSpec((EXPERT_TILE, D_MODEL), lambda i, j: (j, 0)),
                  pl.BlockSpec((D_MODEL, EXPERT_TILE), lambda i, j: (0, j)),
                  dense_spec, dense_spec, dense_spec, dense_spec,
                  pl.BlockSpec((TOKEN_TILE, D_MODEL), lambda i, j: (i, 0))],
        out_specs=pl.BlockSpec((TOKEN_TILE, D_MODEL), lambda i, j: (i, 0)),
        out_shape=jax.ShapeDtypeStruct((tp, D_MODEL), F32),
        scratch_shapes=[pltpu.VMEM((EXPERT_TILE, TOKEN_TILE), F32),
                        pltpu.VMEM((EXPERT_TILE, TOKEN_TILE), BF16),
                        pltpu.VMEM((D_MODEL, TOKEN_TILE), F32)],
        compiler_params=_params("parallel", "arbitrary"),
        name="peer_dense",
    )(xt, u_bf, vt_bf, rank2, e2, nsel, c1, h)


def peer_layer(h, g, wq, sub_keys, u, v):
    wq_t_hi, wq_t_lo = _split(wq.T)
    keys = sub_keys.reshape(2 * PEER_HEADS, PEER_N_KEYS, PEER_KEY_HALF)
    xt, rank2, e2, nsel, c1 = peer_route(h, g, wq_t_hi, wq_t_lo, keys)
    return peer_dense(h, xt, u.astype(BF16), v.T.astype(BF16), rank2, e2, nsel, c1)


def _bucket_starts():
    max_exact = T5_BUCKETS // 2
    d = np.arange(0, 4 * T5_MAX_DIST)
    df = np.maximum(d, 1).astype(np.float64)
    large = max_exact + (np.log(df / max_exact) / math.log(T5_MAX_DIST / max_exact)
                         * (T5_BUCKETS - max_exact)).astype(np.int64)
    bucket = np.where(d < max_exact, d, np.minimum(large, T5_BUCKETS - 1))
    starts = []
    for b in range(T5_BUCKETS):
        idx = np.nonzero(bucket == b)[0]
        if idx.size:
            starts.append((int(idx[0]), b))
    return starts


BUCKET_STARTS = _bucket_starts()


def _bias_from_dist(dist, table):
    out = None
    for start, b in BUCKET_STARTS:
        val = table(b)
        out = val + jnp.zeros(dist.shape, F32) if out is None else jnp.where(dist >= start, val, out)
    return out


def _moba_prompt_kernel(rb_ref, q_ref, k_ref, v_ref, o_ref,
                        kb_scr, vt_scr, km_scr, bias_scr, sel_scr, m_scr, l_scr, acc_scr):
    pair = pl.program_id(1)
    qt = pl.program_id(2)
    n_blk = k_ref.shape[0] // MOBA_BLOCK
    blk = MOBA_BLOCK
    lane = lax.broadcasted_iota(jnp.int32, (1, 2 * HEAD_DIM), 1)

    @pl.when(qt == 0)
    def _():
        for b in range(n_blk):
            kt = k_ref[b * blk:(b + 1) * blk, :]
            kb_scr[b] = kt.astype(BF16)
            km_scr[b:b + 1, :] = jnp.mean(kt, axis=0, keepdims=True)
            vt_scr[b] = v_ref[b * blk:(b + 1) * blk, :].T.astype(BF16)
        ik = lax.broadcasted_iota(jnp.int32, (blk, blk), 0)
        iq = lax.broadcasted_iota(jnp.int32, (blk, blk), 1)
        for hh in range(2):
            head = 2 * pair + hh
            for delta in range(2):
                dist = iq - ik + delta * blk
                bias = _bias_from_dist(jnp.maximum(dist, 0), lambda b: rb_ref[b, head])
                bias_scr[hh, delta] = jnp.where(dist >= 0, bias, NEG_INF)

    own = qt
    blk_iota = lax.broadcasted_iota(jnp.int32, (n_blk, blk), 0)
    for hh in range(2):
        head = 2 * pair + hh
        in_head = (lane >= hh * HEAD_DIM) & (lane < (hh + 1) * HEAD_DIM)
        qh = jnp.where(in_head, q_ref[...] * (HEAD_DIM ** -0.5), 0.0)
        qh_bf = qh.astype(BF16)
        km = jnp.where(in_head, km_scr[...], 0.0)
        gate = lax.dot_general(km, qh, (((1,), (1,)), ((), ())),
                               precision=lax.Precision.HIGHEST,
                               preferred_element_type=F32)
        gate = jnp.where(blk_iota < own, gate, NEG_INF)
        rank = jnp.zeros((n_blk, blk), F32)
        for b in range(n_blk):
            gb = gate[b:b + 1, :]
            ahead = (gb > gate) | ((gb == gate) & (b < blk_iota))
            rank = rank + jnp.where(ahead, 1.0, 0.0)
        sel_scr[...] = jnp.where((blk_iota < own) & (rank < float(MOBA_TOPK)), 1.0, 0.0)
        far_bias = rb_ref[T5_BUCKETS - 1, head]
        v_rows = slice(hh * HEAD_DIM, (hh + 1) * HEAD_DIM)

        def scores(kb):
            return _dot_nt(kb_scr[kb], qh_bf), kb

        def accumulate(s, kb, first):
            m_new = jnp.max(s, axis=0, keepdims=True)
            if not first:
                m_new = jnp.maximum(m_scr[...], m_new)
            p = jnp.exp(s - m_new)
            pv = _dot(vt_scr[kb, v_rows, :], p.astype(BF16))
            if first:
                l_scr[...] = jnp.sum(p, axis=0, keepdims=True)
                acc_scr[v_rows, :] = pv
            else:
                alpha = jnp.exp(m_scr[...] - m_new)
                l_scr[...] = alpha * l_scr[...] + jnp.sum(p, axis=0, keepdims=True)
                acc_scr[v_rows, :] = alpha * acc_scr[v_rows, :] + pv
            m_scr[...] = m_new

        s, start = scores(own)
        accumulate(s + bias_scr[hh, 0], start, True)

        @pl.when(own >= 1)
        def _():
            s, start = scores(own - 1)
            s = jnp.where(sel_scr[pl.ds(own - 1, 1), :] > 0.0, s + bias_scr[hh, 1], NEG_INF)
            accumulate(s, start, False)

        def far(kb, carry):
            s, start = scores(kb)
            s = jnp.where(sel_scr[pl.ds(kb, 1), :] > 0.0, s + far_bias, NEG_INF)
            accumulate(s, start, False)
            return carry

        lax.fori_loop(0, jnp.maximum(own - 1, 0), far, 0)
        acc_scr[v_rows, :] = acc_scr[v_rows, :] / l_scr[...]
    o_ref[...] = acc_scr[...].T


def moba_prompt(q, k, v, rel_bias, n_b, seq):
    assert seq % MOBA_BLOCK == 0
    nq = seq // MOBA_BLOCK
    width = 2 * HEAD_DIM
    grid_spec = pltpu.PrefetchScalarGridSpec(
        num_scalar_prefetch=0,
        grid=(n_b, N_HEADS // 2, nq),
        in_specs=[pl.BlockSpec(memory_space=pltpu.SMEM),
                  pl.BlockSpec((MOBA_BLOCK, width), lambda n, p, t: (n * nq + t, p)),
                  pl.BlockSpec((seq, width), lambda n, p, t: (n, p)),
                  pl.BlockSpec((seq, width), lambda n, p, t: (n, p))],
        out_specs=pl.BlockSpec((MOBA_BLOCK, width), lambda n, p, t: (n * nq + t, p)),
        scratch_shapes=[pltpu.VMEM((nq, MOBA_BLOCK, width), BF16),
                        pltpu.VMEM((nq, width, MOBA_BLOCK), BF16),
                        pltpu.VMEM((nq, width), F32),
                        pltpu.VMEM((2, 2, MOBA_BLOCK, MOBA_BLOCK), F32),
                        pltpu.VMEM((nq, MOBA_BLOCK), F32),
                        pltpu.VMEM((1, MOBA_BLOCK), F32),
                        pltpu.VMEM((1, MOBA_BLOCK), F32),
                        pltpu.VMEM((width, MOBA_BLOCK), F32)],
    )
    return pl.pallas_call(
        _moba_prompt_kernel,
        grid_spec=grid_spec,
        out_shape=jax.ShapeDtypeStruct((n_b * seq, D_MODEL), F32),
        compiler_params=_params("parallel", "parallel", "arbitrary"),
        name="moba_prompt",
    )(rel_bias, q, k, v)


def _moba_sample_kernel(pt_ref, rbt_ref, q_ref, kn_ref, vn_ref, k0_ref, k1_ref, v0_ref, v1_ref,
                        o_ref, qbd_scr, gate_scr, m_scr, l_scr, acc_scr):
    b = pl.program_id(1)
    n_blk = pl.num_programs(1)
    blk = MOBA_BLOCK
    head_of_lane = lax.broadcasted_iota(jnp.int32, (N_HEADS, D_MODEL), 1) // HEAD_DIM
    head_of_row = lax.broadcasted_iota(jnp.int32, (N_HEADS, D_MODEL), 0)
    diag = head_of_lane == head_of_row

    @pl.when(b == 0)
    def _():
        qbd_scr[...] = jnp.where(diag, q_ref[0] * (HEAD_DIM ** -0.5), 0.0)

    qbd = qbd_scr[...]
    k_blk = jnp.concatenate([k0_ref[0], k1_ref[0]], axis=0)
    v_blk = jnp.concatenate([v0_ref[0], v1_ref[0]], axis=0)
    stat = (N_HEADS, STAT_LANES)
    gate = jnp.sum(qbd * jnp.mean(k_blk, axis=0, keepdims=True), axis=1, keepdims=True)
    gate_scr[b] = jnp.broadcast_to(gate, stat)
    s = _dot_nt(qbd.astype(BF16), k_blk.astype(BF16))
    dist = (n_blk - b) * blk - lax.broadcasted_iota(jnp.int32, (N_HEADS, blk), 1)
    s = s + _bias_from_dist(dist, lambda t: rbt_ref[:, t:t + 1])
    m = jnp.max(s, axis=1, keepdims=True)
    p = jnp.exp(s - m)
    m_scr[b] = jnp.broadcast_to(m, stat)
    l_scr[b] = jnp.broadcast_to(jnp.sum(p, axis=1, keepdims=True), stat)
    acc_scr[b] = _dot(p.astype(BF16), v_blk.astype(BF16))

    @pl.when(b == n_blk - 1)
    def _():
        n_past = m_scr.shape[0]
        gates = [gate_scr[i] for i in range(n_past)]
        s_self = jnp.sum(qbd * kn_ref[0], axis=1, keepdims=True) + rbt_ref[:, 0:1]
        m_tot = jnp.broadcast_to(s_self, stat)
        keep = []
        for i in range(n_past):
            rank = jnp.zeros(stat, F32)
            for i2 in range(n_past):
                if i2 == i:
                    continue
                ahead = (gates[i2] > gates[i]) if i2 > i else (gates[i2] >= gates[i])
                rank = rank + jnp.where(ahead, 1.0, 0.0)
            keep.append(rank < float(MOBA_TOPK))
            m_tot = jnp.where(keep[i], jnp.maximum(m_tot, m_scr[i]), m_tot)
        w_self = jnp.exp(s_self - m_tot)
        l_tot = w_self
        out = w_self[:, 0:1] * vn_ref[0]
        for i in range(n_past):
            w_i = jnp.where(keep[i], jnp.exp(m_scr[i] - m_tot), 0.0)
            l_tot = l_tot + w_i * l_scr[i]
            out = out + w_i[:, 0:1] * acc_scr[i]
        out = jnp.where(diag, out / l_tot[:, 0:1], 0.0)
        o_ref[0] = jnp.sum(out, axis=0, keepdims=True)


def moba_sample(q, k_new, v_new, cache_k, cache_v, page_table, rel_bias):
    n_d, n_pages = page_table.shape
    pages_per_blk = MOBA_BLOCK // PAGE_SIZE
    assert pages_per_blk == 2 and n_pages % pages_per_blk == 0
    n_blk = n_pages // pages_per_blk
    n_phys = cache_k.shape[0]
    ck = cache_k.reshape(n_phys, PAGE_SIZE, D_MODEL)
    cv = cache_v.reshape(n_phys, PAGE_SIZE, D_MODEL)
    row = pl.BlockSpec((1, 1, D_MODEL), lambda n, b, pt: (n, 0, 0))

    def page(which):
        return pl.BlockSpec((1, PAGE_SIZE, D_MODEL),
                            lambda n, b, pt: (pt[n, pages_per_blk * b + which], 0, 0))

    grid_spec = pltpu.PrefetchScalarGridSpec(
        num_scalar_prefetch=1,
        grid=(n_d, n_blk),
        in_specs=[pl.BlockSpec((N_HEADS, T5_BUCKETS), lambda n, b, pt: (0, 0)),
                  row, row, row, page(0), page(1), page(0), page(1)],
        out_specs=row,
        scratch_shapes=[pltpu.VMEM((N_HEADS, D_MODEL), F32),
                        pltpu.VMEM((n_blk, N_HEADS, STAT_LANES), F32),
                        pltpu.VMEM((n_blk, N_HEADS, STAT_LANES), F32),
                        pltpu.VMEM((n_blk, N_HEADS, STAT_LANES), F32),
                        pltpu.VMEM((n_blk, N_HEADS, D_MODEL), F32)],
    )
    out = pl.pallas_call(
        _moba_sample_kernel,
        grid_spec=grid_spec,
        out_shape=jax.ShapeDtypeStruct((n_d, 1, D_MODEL), F32),
        compiler_params=_params("parallel", "arbitrary"),
        name="moba_sample",
    )(page_table, rel_bias.T, q[:, None, :], k_new[:, None, :], v_new[:, None, :], ck, ck, cv, cv)
    return out[:, 0, :]


def kernel(x_prompt, x_sample, state_conv, cache_k, cache_v, page_table, p_prompt, p_sample, rel_bias, norm_mix_g, norm_ffn_g, norm_ple_g, norm_final_g, conv_w_in, conv_b_in, conv_dw_w, conv_dw_b, conv_ln_g, conv_ln_b, conv_w_out, conv_b_out, attn_w_qkv, attn_w_o, peer_w_q, peer_sub_keys, peer_u, peer_v, ple_w_proj, ple_w_gate):
    n_b, seq, d = x_prompt.shape
    n_d, dec_seq, _ = x_sample.shape
    assert d == D_MODEL and dec_seq == 1 and seq % TOKEN_TILE == 0
    depth = norm_mix_g.shape[0]
    n_p = n_b * seq
    tp = -(-(n_p + n_d) // TOKEN_TILE) * TOKEN_TILE
    pad = tp - n_p - n_d

    def pack(prompt_rows, sample_rows):
        tail = jnp.zeros((pad, prompt_rows.shape[1]), prompt_rows.dtype)
        return jnp.concatenate([prompt_rows, sample_rows.astype(prompt_rows.dtype), tail], axis=0)

    def row(vec):
        return vec.reshape(1, -1)

    h = pack(x_prompt.reshape(n_p, d), x_sample.reshape(n_d, d))
    conv_p, conv_s, kp, vp, ks, vs = [], [], [], [], [], []
    zero_bias = jnp.zeros((1, d), F32)
    for i in range(depth):
        j = i // 2
        if i % 2 == 0:
            glu = glu_in(h, row(norm_mix_g[i]), conv_w_in[j].astype(BF16), row(conv_b_in[j]))
            dwk = jnp.concatenate([conv_dw_w[j][::-1], jnp.zeros((1, d), F32)], axis=0)
            y_p, tail = conv_prompt(glu, n_b, seq, dwk, row(conv_dw_b[j]),
                                    row(conv_ln_g[j]), row(conv_ln_b[j]))
            glu_s = glu[n_p:n_p + n_d]
            y_s = conv_sample(glu_s, state_conv[j], conv_dw_w[j], row(conv_dw_b[j]),
                              row(conv_ln_g[j]), row(conv_ln_b[j]))
            conv_p.append(tail[:, HALO - (CONV_WIDTH - 1):, :])
            conv_s.append(jnp.concatenate([state_conv[j][:, 1:, :], glu_s[:, None, :]], axis=1))
            h = proj_residual(h, pack(y_p, y_s), conv_w_out[j].astype(BF16), row(conv_b_out[j]))
        else:
            wq_hi, wq_lo = _split(attn_w_qkv[j][:, :d])
            q, k, v = qkv_proj(h, row(norm_mix_g[i]), wq_hi, wq_lo, attn_w_qkv[j][:, d:].astype(BF16))
            o_p = moba_prompt(q, k, v, rel_bias, n_b, seq)
            o_s = moba_sample(q[n_p:n_p + n_d], k[n_p:n_p + n_d], v[n_p:n_p + n_d],
                              cache_k[j], cache_v[j], page_table, rel_bias)
            kp.append(k[:n_p].reshape(n_b, seq, N_HEADS, HEAD_DIM))
            vp.append(v[:n_p].reshape(n_b, seq, N_HEADS, HEAD_DIM))
            ks.append(k[n_p:n_p + n_d].reshape(n_d, 1, N_HEADS, HEAD_DIM))
            vs.append(v[n_p:n_p + n_d].reshape(n_d, 1, N_HEADS, HEAD_DIM))
            h = proj_residual(h, pack(o_p, o_s), attn_w_o[j].astype(BF16), zero_bias)
        h = peer_layer(h, row(norm_ffn_g[i]), peer_w_q[i], peer_sub_keys[i], peer_u[i], peer_v[i])
        p_all = pack(p_prompt[i].reshape(n_p, -1), p_sample[i].reshape(n_d, -1))
        h = ple(h, p_all, row(norm_ple_g[i]), ple_w_gate[i].astype(BF16),
                ple_w_proj[i].astype(BF16), row(norm_final_g), final_norm=(i == depth - 1))
    y_prompt = h[:n_p].reshape(n_b, seq, d)
    y_sample = h[n_p:n_p + n_d].reshape(n_d, 1, d)
    return (y_prompt, y_sample, jnp.stack(conv_p), jnp.stack(conv_s),
            jnp.stack(kp), jnp.stack(vp), jnp.stack(ks), jnp.stack(vs))
```

```python
import functools
import math

import numpy as np
import jax
import jax.numpy as jnp
from jax import lax
from jax.experimental import pallas as pl
from jax.experimental.pallas import tpu as pltpu

F32 = jnp.float32
BF16 = jnp.bfloat16
EPS = 1e-6
NEG_INF = float("-inf")

D_MODEL = 1024
N_HEADS = 16
HEAD_DIM = 64
CONV_WIDTH = 31
MOBA_BLOCK = 256
MOBA_TOPK = 3
T5_BUCKETS = 32
T5_MAX_DIST = 128
PEER_HEADS = 8
PEER_N_KEYS = 128
PEER_KEY_HALF = 64
PEER_TOPK = 16
PAGE_SIZE = 128

TOKEN_TILE = 512
ROUTE_TILE = 256
ROUTE_HEADS = 2
EXPERT_TILE = 1024
CONV_CHUNK = 64
HALO = 32
STAT_LANES = 128
LANES = 128
BF16_ROWS = 16
VMEM_LIMIT = 56 * 1024 * 1024

CAND_COUNT = [PEER_TOPK // (j + 1) for j in range(PEER_TOPK)]
CAND_ROWS = [16] + [8] * (PEER_TOPK - 1)
CAND_TOTAL = sum(CAND_ROWS)


def _params(*sem):
    return pltpu.CompilerParams(dimension_semantics=sem, vmem_limit_bytes=VMEM_LIMIT)


def _dot(a, b):
    return jnp.dot(a, b, preferred_element_type=F32)


def _dot_nt(a, b):
    return lax.dot_general(a, b, (((1,), (1,)), ((), ())), preferred_element_type=F32)


def _split(x):
    hi = x.astype(BF16)
    lo = (x - hi.astype(F32)).astype(BF16)
    return hi, lo


def _dot3(a_hi, a_lo, b_hi, b_lo):
    return _dot(a_hi, b_hi) + _dot(a_hi, b_lo) + _dot(a_lo, b_hi)


def _rms(x, g):
    return x * lax.rsqrt(jnp.mean(x * x, axis=-1, keepdims=True) + EPS) * g


def _sigmoid(x):
    return 1.0 / (1.0 + jnp.exp(-x))


def _row_spec(tile, width):
    return pl.BlockSpec((tile, width), lambda i: (i, 0))


def _full_spec(shape):
    nd = len(shape)
    return pl.BlockSpec(shape, lambda *_: (0,) * nd)


def _glu_in_kernel(h_ref, g_ref, w_ref, b_ref, o_ref):
    a = _rms(h_ref[...], g_ref[...]).astype(BF16)
    z = _dot(a, w_ref[...]) + b_ref[...]
    o_ref[...] = z[:, :D_MODEL] * _sigmoid(z[:, D_MODEL:])


def glu_in(h, g, w_bf, b):
    tp = h.shape[0]
    return pl.pallas_call(
        _glu_in_kernel,
        grid=(tp // TOKEN_TILE,),
        in_specs=[_row_spec(TOKEN_TILE, D_MODEL), _full_spec((1, D_MODEL)),
                  _full_spec((D_MODEL, 2 * D_MODEL)), _full_spec((1, 2 * D_MODEL))],
        out_specs=_row_spec(TOKEN_TILE, D_MODEL),
        out_shape=jax.ShapeDtypeStruct((tp, D_MODEL), F32),
        compiler_params=_params("parallel"),
        name="glu_in",
    )(h, g, w_bf, b)


def _proj_residual_kernel(h_ref, y_ref, w_ref, b_ref, o_ref):
    o_ref[...] = h_ref[...] + _dot(y_ref[...].astype(BF16), w_ref[...]) + b_ref[...]


def proj_residual(h, y, w_bf, b):
    tp = h.shape[0]
    return pl.pallas_call(
        _proj_residual_kernel,
        grid=(tp // TOKEN_TILE,),
        in_specs=[_row_spec(TOKEN_TILE, D_MODEL), _row_spec(TOKEN_TILE, D_MODEL),
                  _full_spec((D_MODEL, D_MODEL)), _full_spec((1, D_MODEL))],
        out_specs=_row_spec(TOKEN_TILE, D_MODEL),
        out_shape=jax.ShapeDtypeStruct((tp, D_MODEL), F32),
        compiler_params=_params("parallel"),
        name="proj_residual",
    )(h, y, w_bf, b)


def _ple_kernel(h_ref, p_ref, g_ref, wg_ref, wp_ref, gf_ref, o_ref, *, final_norm):
    h = h_ref[...]
    gate = _sigmoid(_dot(_rms(h, g_ref[...]).astype(BF16), wg_ref[...]))
    out = h + gate * _dot(p_ref[...].astype(BF16), wp_ref[...])
    if final_norm:
        out = _rms(out, gf_ref[...])
    o_ref[...] = out


def ple(h, p, g, wg_bf, wp_bf, g_final, final_norm):
    tp = h.shape[0]
    pdim = p.shape[1]
    return pl.pallas_call(
        functools.partial(_ple_kernel, final_norm=final_norm),
        grid=(tp // TOKEN_TILE,),
        in_specs=[_row_spec(TOKEN_TILE, D_MODEL), _row_spec(TOKEN_TILE, pdim),
                  _full_spec((1, D_MODEL)), _full_spec((D_MODEL, D_MODEL)),
                  _full_spec((pdim, D_MODEL)), _full_spec((1, D_MODEL))],
        out_specs=_row_spec(TOKEN_TILE, D_MODEL),
        out_shape=jax.ShapeDtypeStruct((tp, D_MODEL), F32),
        compiler_params=_params("parallel"),
        name="ple_final" if final_norm else "ple",
    )(h, p, g, wg_bf, wp_bf, g_final)


def _qkv_kernel(h_ref, g_ref, wq_hi_ref, wq_lo_ref, wkv_ref, q_ref, k_ref, v_ref):
    a_hi, a_lo = _split(_rms(h_ref[...], g_ref[...]))
    q_ref[...] = _dot3(a_hi, a_lo, wq_hi_ref[...], wq_lo_ref[...])
    kv = _dot(a_hi, wkv_ref[...])
    k_ref[...] = kv[:, :D_MODEL]
    v_ref[...] = kv[:, D_MODEL:]


def qkv_proj(h, g, wq_hi, wq_lo, wkv_bf):
    tp = h.shape[0]
    out = jax.ShapeDtypeStruct((tp, D_MODEL), F32)
    return pl.pallas_call(
        _qkv_kernel,
        grid=(tp // TOKEN_TILE,),
        in_specs=[_row_spec(TOKEN_TILE, D_MODEL), _full_spec((1, D_MODEL)),
                  _full_spec((D_MODEL, D_MODEL)), _full_spec((D_MODEL, D_MODEL)),
                  _full_spec((D_MODEL, 2 * D_MODEL))],
        out_specs=[_row_spec(TOKEN_TILE, D_MODEL)] * 3,
        out_shape=[out, out, out],
        compiler_params=_params("parallel"),
        name="qkv_proj",
    )(h, g, wq_hi, wq_lo, wkv_bf)


def _ln_silu(y, g, b):
    mu = jnp.mean(y, axis=-1, keepdims=True)
    yc = y - mu
    y = yc * lax.rsqrt(jnp.mean(yc * yc, axis=-1, keepdims=True) + EPS) * g + b
    return y * _sigmoid(y)


def _conv_prompt_kernel(x_ref, dwk_ref, dwb_ref, lng_ref, lnb_ref, y_ref, st_ref, buf_ref):
    tile = x_ref.shape[0]

    @pl.when(pl.program_id(1) == 0)
    def _():
        buf_ref[0:HALO, :] = jnp.zeros((HALO, D_MODEL), F32)

    buf_ref[HALO:HALO + tile, :] = x_ref[...]

    def chunk(c, carry):
        t0 = pl.multiple_of(c * CONV_CHUNK, CONV_CHUNK)
        win = buf_ref[pl.ds(t0, CONV_CHUNK + HALO), :]
        acc = None
        for s in range(8):
            part = None
            for u in range(4):
                k = 8 * u + s
                term = win[24 - 8 * u:24 - 8 * u + CONV_CHUNK + 8, :] * dwk_ref[k:k + 1, :]
                part = term if part is None else part + term
            if s:
                part = pltpu.roll(part, s, 0)
            part = part[8:8 + CONV_CHUNK, :]
            acc = part if acc is None else acc + part
        y = _ln_silu(acc + dwb_ref[...], lng_ref[...], lnb_ref[...])
        y_ref[pl.ds(t0, CONV_CHUNK), :] = y.astype(y_ref.dtype)
        return carry

    lax.fori_loop(0, tile // CONV_CHUNK, chunk, 0)
    tail = buf_ref[tile:tile + HALO, :]
    st_ref[0] = tail
    buf_ref[0:HALO, :] = tail


def conv_prompt(glu, n_b, seq, dwk, dwb, lng, lnb):
    nt = seq // TOKEN_TILE
    return pl.pallas_call(
        _conv_prompt_kernel,
        grid=(n_b, nt),
        in_specs=[pl.BlockSpec((TOKEN_TILE, D_MODEL), lambda n, t: (n * nt + t, 0)),
                  _full_spec((32, D_MODEL)), _full_spec((1, D_MODEL)),
                  _full_spec((1, D_MODEL)), _full_spec((1, D_MODEL))],
        out_specs=[pl.BlockSpec((TOKEN_TILE, D_MODEL), lambda n, t: (n * nt + t, 0)),
                   pl.BlockSpec((1, HALO, D_MODEL), lambda n, t: (n, 0, 0))],
        out_shape=[jax.ShapeDtypeStruct((n_b * seq, D_MODEL), BF16),
                   jax.ShapeDtypeStruct((n_b, HALO, D_MODEL), F32)],
        scratch_shapes=[pltpu.VMEM((TOKEN_TILE + HALO, D_MODEL), F32)],
        compiler_params=_params("arbitrary", "arbitrary"),
        name="conv_prompt",
    )(glu, dwk, dwb, lng, lnb)


def _conv_sample_kernel(x_ref, st_ref, dw_ref, dwb_ref, lng_ref, lnb_ref, y_ref):
    hist = jnp.sum(st_ref[...] * dw_ref[0:CONV_WIDTH - 1, :][None], axis=1)
    y = hist + x_ref[...] * dw_ref[CONV_WIDTH - 1:CONV_WIDTH, :] + dwb_ref[...]
    y_ref[...] = _ln_silu(y, lng_ref[...], lnb_ref[...]).astype(y_ref.dtype)


def conv_sample(glu_s, state, dw, dwb, lng, lnb):
    n_d = glu_s.shape[0]
    blk = 32 if n_d % 32 == 0 else n_d
    return pl.pallas_call(
        _conv_sample_kernel,
        grid=(n_d // blk,),
        in_specs=[_row_spec(blk, D_MODEL),
                  pl.BlockSpec((blk, CONV_WIDTH - 1, D_MODEL), lambda i: (i, 0, 0)),
                  _full_spec((CONV_WIDTH, D_MODEL)), _full_spec((1, D_MODEL)),
                  _full_spec((1, D_MODEL)), _full_spec((1, D_MODEL))],
        out_specs=_row_spec(blk, D_MODEL),
        out_shape=jax.ShapeDtypeStruct((n_d, D_MODEL), BF16),
        compiler_params=_params("parallel"),
        name="conv_sample",
    )(glu_s, state, dw, dwb, lng, lnb)


def _extract_top(w, tie_key, vals_ref, first_ref):
    def body(r, w):
        m = jnp.max(w, axis=1, keepdims=True)
        first = jnp.min(jnp.where(w == m, tie_key, 1e9), axis=1, keepdims=True)
        for g in range(w.shape[0]):
            if vals_ref is not None:
                vals_ref[g, pl.ds(r, 1), :] = m[g]
            if first_ref is not None:
                first_ref[g, pl.ds(r, 1), :] = first[g]
        return jnp.where(tie_key == first, NEG_INF, w)

    return lax.fori_loop(0, PEER_TOPK, body, w)


def _peer_route_kernel(x_ref, g_ref, wq_hi_ref, wq_lo_ref, keys_ref, cpos_ref, cneg_ref,
                       xt_ref, rank2_ref, e2_ref, nsel_ref, c1_ref,
                       q_scr, s_scr, v_scr, i_scr, c_scr):
    tile = x_ref.shape[0]
    xn_t = _rms(x_ref[...], g_ref[...]).T
    x_hi, x_lo = _split(xn_t)
    xt_ref[...] = x_hi
    q_scr[...] = (_dot(wq_hi_ref[...], x_hi) + _dot(wq_hi_ref[...], x_lo)
                  + _dot(wq_lo_ref[...], x_hi))
    key_iota = lax.broadcasted_iota(jnp.int32, (PEER_N_KEYS, tile), 0).astype(F32)
    cpos = cpos_ref[...]

    def head_group(hg, carry):
        for g in range(2 * ROUTE_HEADS):
            i = 2 * ROUTE_HEADS * hg + g
            qs = q_scr[pl.ds(pl.multiple_of(i * PEER_KEY_HALF, PEER_KEY_HALF), PEER_KEY_HALF), :]
            s_scr[g] = jnp.dot(keys_ref[i], qs, precision=lax.Precision.HIGHEST,
                               preferred_element_type=F32)
        _extract_top(s_scr[...], key_iota, v_scr, i_scr)
        for hh in range(ROUTE_HEADS):
            v2_16 = v_scr[2 * hh + 1]
            v2_8 = v_scr[2 * hh + 1, 0:8, :]
            c_scr[hh] = jnp.concatenate(
                [v_scr[2 * hh, j:j + 1, :] + (v2_16 if j == 0 else v2_8) for j in range(PEER_TOPK)],
                axis=0) + cneg_ref[...]
        left = _extract_top(c_scr[...], cpos, None, None)
        for hh in range(ROUTE_HEADS):
            h = ROUTE_HEADS * hg + hh
            max1 = v_scr[2 * hh, 0:1, :]
            max2 = v_scr[2 * hh + 1, 0:1, :]
            cand = c_scr[hh]
            picked = jnp.where((left[hh] == NEG_INF) & (cand > NEG_INF), 1.0, 0.0)
            z = jnp.sum(picked * jnp.exp(cand - (max1 + max2)), axis=0, keepdims=True)
            nsel = jnp.zeros((PEER_N_KEYS, tile), F32)
            rank2 = jnp.full((PEER_N_KEYS, tile), float(PEER_TOPK), F32)
            row = 0
            for j in range(PEER_TOPK):
                n_j = jnp.sum(picked[row:row + CAND_ROWS[j], :], axis=0, keepdims=True)
                nsel = jnp.where(key_iota == i_scr[2 * hh, j:j + 1, :], n_j, nsel)
                rank2 = jnp.where(key_iota == i_scr[2 * hh + 1, j:j + 1, :], float(j), rank2)
                row += CAND_ROWS[j]
            rank2_ref[h] = rank2.astype(rank2_ref.dtype)
            nsel_ref[h] = nsel
            e2_ref[h] = jnp.exp(s_scr[2 * hh + 1] - max2).astype(e2_ref.dtype)
            c1_ref[h] = jnp.exp(s_scr[2 * hh] - max1) / z
        return carry

    lax.fori_loop(0, PEER_HEADS // ROUTE_HEADS, head_group, 0)


def _cand_constants(tile):
    pos, neg = [], []
    for j in range(PEER_TOPK):
        for l in range(CAND_ROWS[j]):
            pos.append(16.0 * j + l)
            neg.append(0.0 if l < CAND_COUNT[j] else NEG_INF)
    pos = np.broadcast_to(np.asarray(pos, np.float32)[:, None], (CAND_TOTAL, tile))
    neg = np.broadcast_to(np.asarray(neg, np.float32)[:, None], (CAND_TOTAL, tile))
    return jnp.asarray(pos), jnp.asarray(neg)


def peer_route(h, g, wq_t_hi, wq_t_lo, keys):
    tp = h.shape[0]
    cpos, cneg = _cand_constants(ROUTE_TILE)
    dense = jax.ShapeDtypeStruct((PEER_HEADS, PEER_N_KEYS, tp), F32)
    dense_bf = jax.ShapeDtypeStruct((PEER_HEADS, PEER_N_KEYS, tp), BF16)
    dense_spec = pl.BlockSpec((PEER_HEADS, PEER_N_KEYS, ROUTE_TILE), lambda i: (0, 0, i))
    return pl.pallas_call(
        _peer_route_kernel,
        grid=(tp // ROUTE_TILE,),
        in_specs=[_row_spec(ROUTE_TILE, D_MODEL), _full_spec((1, D_MODEL)),
                  _full_spec((D_MODEL, D_MODEL)), _full_spec((D_MODEL, D_MODEL)),
                  _full_spec((2 * PEER_HEADS, PEER_N_KEYS, PEER_KEY_HALF)),
                  _full_spec((CAND_TOTAL, ROUTE_TILE)), _full_spec((CAND_TOTAL, ROUTE_TILE))],
        out_specs=[pl.BlockSpec((D_MODEL, ROUTE_TILE), lambda i: (0, i)),
                   dense_spec, dense_spec, dense_spec, dense_spec],
        out_shape=[jax.ShapeDtypeStruct((D_MODEL, tp), BF16), dense_bf, dense_bf, dense, dense],
        scratch_shapes=[pltpu.VMEM((D_MODEL, ROUTE_TILE), F32),
                        pltpu.VMEM((2 * ROUTE_HEADS, PEER_N_KEYS, ROUTE_TILE), F32),
                        pltpu.VMEM((2 * ROUTE_HEADS, PEER_TOPK, ROUTE_TILE), F32),
                        pltpu.VMEM((2 * ROUTE_HEADS, PEER_TOPK, ROUTE_TILE), F32),
                        pltpu.VMEM((ROUTE_HEADS, CAND_TOTAL, ROUTE_TILE), F32)],
        compiler_params=_params("parallel"),
        name="peer_route",
    )(h, g, wq_t_hi, wq_t_lo, keys, cpos, cneg)


def _peer_dense_kernel(xt_ref, u_ref, v_ref, rank2_in, e2_in, nsel_ref, c1_ref, h_ref,
                       o_ref, act_scr, w_scr, acc_scr, rank2_ref, e2_ref):
    j = pl.program_id(1)

    @pl.when(j == 0)
    def _():
        acc_scr[...] = jnp.zeros(acc_scr.shape, F32)
        rank2_ref[...] = rank2_in[...]
        e2_ref[...] = e2_in[...]

    act_scr[...] = _dot(u_ref[...], xt_ref[...])
    pack = BF16_ROWS
    zero = jnp.zeros((pack, LANES), BF16)
    for a_local in range(EXPERT_TILE // PEER_N_KEYS):
        a_row = slice(a_local, a_local + 1)
        for col in range(xt_ref.shape[1] // LANES):
            cols = slice(col * LANES, (col + 1) * LANES)
            gate = None
            for h in range(PEER_HEADS):
                n_row = jnp.broadcast_to(nsel_ref[h, a_row, cols], (pack, LANES)).astype(BF16)
                c_row = jnp.broadcast_to(c1_ref[h, a_row, cols], (pack, LANES)).astype(BF16)
                n_all = jnp.concatenate([n_row] * (PEER_N_KEYS // pack), axis=0)
                c_all = jnp.concatenate([c_row] * (PEER_N_KEYS // pack), axis=0)
                term = jnp.where(rank2_ref[h, :, cols] < n_all, e2_ref[h, :, cols] * c_all, zero[0:1])
                gate = term if gate is None else gate + term
            rows = slice(a_local * PEER_N_KEYS, (a_local + 1) * PEER_N_KEYS)
            act = act_scr[rows, cols]
            gelu = 0.5 * act * (1.0 + lax.erf(act * (1.0 / math.sqrt(2.0))))
            w_scr[rows, cols] = gelu.astype(BF16) * gate
    acc_scr[...] += lax.dot_general(w_scr[...], v_ref[...], (((0,), (0,)), ((), ())),
                                    preferred_element_type=F32)

    @pl.when(j == pl.num_programs(1) - 1)
    def _():
        o_ref[...] = h_ref[...] + acc_scr[...]


def peer_dense(h, xt, u_bf, v_bf, rank2, e2, nsel, c1):
    tp = h.shape[0]
    n_exp = u_bf.shape[0]
    dense_spec = pl.BlockSpec((PEER_HEADS, PEER_N_KEYS, TOKEN_TILE), lambda i, j: (0, 0, i))
    a_spec = pl.BlockSpec((PEER_HEADS, EXPERT_TILE // PEER_N_KEYS, TOKEN_TILE),
                          lambda i, j: (0, j, i))
    return pl.pallas_call(
        _peer_dense_kernel,
        grid=(tp // TOKEN_TILE, n_exp // EXPERT_TILE),
        in_specs=[pl.BlockSpec((D_MODEL, TOKEN_TILE), lambda i, j: (0, i)),
                  pl.BlockSpec((EXPERT_TILE, D_MODEL), lambda i, j: (j, 0)),
                  pl.BlockSpec((EXPERT_TILE, D_MODEL), lambda i, j: (j, 0)),
                  dense_spec, dense_spec, a_spec, a_spec,
                  pl.BlockSpec((TOKEN_TILE, D_MODEL), lambda i, j: (i, 0))],
        out_specs=pl.BlockSpec((TOKEN_TILE, D_MODEL), lambda i, j: (i, 0)),
        out_shape=jax.ShapeDtypeStruct((tp, D_MODEL), F32),
        scratch_shapes=[pltpu.VMEM((EXPERT_TILE, TOKEN_TILE), F32),
                        pltpu.VMEM((EXPERT_TILE, TOKEN_TILE), BF16),
                        pltpu.VMEM((TOKEN_TILE, D_MODEL), F32),
                        pltpu.VMEM((PEER_HEADS, PEER_N_KEYS, TOKEN_TILE), BF16),
                        pltpu.VMEM((PEER_HEADS, PEER_N_KEYS, TOKEN_TILE), BF16)],
        compiler_params=_params("parallel", "arbitrary"),
        name="peer_dense",
    )(xt, u_bf, v_bf, rank2, e2, nsel, c1, h)


def peer_layer(h, g, wq, sub_keys, u, v):
    wq_t_hi, wq_t_lo = _split(wq.T)
    keys = sub_keys.reshape(2 * PEER_HEADS, PEER_N_KEYS, PEER_KEY_HALF)
    xt, rank2, e2, nsel, c1 = peer_route(h, g, wq_t_hi, wq_t_lo, keys)
    return peer_dense(h, xt, u.astype(BF16), v.astype(BF16), rank2, e2, nsel, c1)


def _bucket_starts():
    max_exact = T5_BUCKETS // 2
    d = np.arange(0, 4 * T5_MAX_DIST)
    df = np.maximum(d, 1).astype(np.float64)
    large = max_exact + (np.log(df / max_exact) / math.log(T5_MAX_DIST / max_exact)
                         * (T5_BUCKETS - max_exact)).astype(np.int64)
    bucket = np.where(d < max_exact, d, np.minimum(large, T5_BUCKETS - 1))
    starts = []
    for b in range(T5_BUCKETS):
        idx = np.nonzero(bucket == b)[0]
        if idx.size:
            starts.append((int(idx[0]), b))
    return starts


BUCKET_STARTS = _bucket_starts()


def _bias_from_dist(dist, table):
    out = None
    for start, b in BUCKET_STARTS:
        val = table(b)
        out = val + jnp.zeros(dist.shape, F32) if out is None else jnp.where(dist >= start, val, out)
    return out


def _moba_prompt_kernel(rb_ref, q_ref, k_ref, v_ref, o_ref,
                        kb_scr, vt_scr, km_scr, bias_scr, sel_scr, m_scr, l_scr, acc_scr):
    pair = pl.program_id(1)
    qt = pl.program_id(2)
    n_blk = k_ref.shape[0] // MOBA_BLOCK
    blk = MOBA_BLOCK
    lane = lax.broadcasted_iota(jnp.int32, (1, 2 * HEAD_DIM), 1)

    @pl.when(qt == 0)
    def _():
        for b in range(n_blk):
            kt = k_ref[b * blk:(b + 1) * blk, :]
            kb_scr[b] = kt.astype(BF16)
            km_scr[b:b + 1, :] = jnp.mean(kt, axis=0, keepdims=True)
            vt_scr[b] = v_ref[b * blk:(b + 1) * blk, :].T.astype(BF16)
        ik = lax.broadcasted_iota(jnp.int32, (blk, blk), 0)
        iq = lax.broadcasted_iota(jnp.int32, (blk, blk), 1)
        for hh in range(2):
            head = 2 * pair + hh
            for delta in range(2):
                dist = iq - ik + delta * blk
                bias = _bias_from_dist(jnp.maximum(dist, 0), lambda b: rb_ref[b, head])
                bias_scr[hh, delta] = jnp.where(dist >= 0, bias, NEG_INF)

    own = qt
    blk_iota = lax.broadcasted_iota(jnp.int32, (n_blk, blk), 0)
    heads = range(2)
    qh_bf, far_bias = [], []
    for hh in heads:
        head = 2 * pair + hh
        in_head = (lane >= hh * HEAD_DIM) & (lane < (hh + 1) * HEAD_DIM)
        qh = jnp.where(in_head, q_ref[...] * (HEAD_DIM ** -0.5), 0.0)
        qh_bf.append(qh.astype(BF16))
        km = jnp.where(in_head, km_scr[...], 0.0)
        gate = lax.dot_general(km, qh, (((1,), (1,)), ((), ())),
                               precision=lax.Precision.HIGHEST,
                               preferred_element_type=F32)
        gate = jnp.where(blk_iota < own, gate, NEG_INF)
        rank = jnp.zeros((n_blk, blk), F32)
        for b in range(n_blk):
            gb = gate[b:b + 1, :]
            ahead = (gb > gate) | ((gb == gate) & (b < blk_iota))
            rank = rank + jnp.where(ahead, 1.0, 0.0)
        sel_scr[hh] = jnp.where((blk_iota < own) & (rank < float(MOBA_TOPK)), 1.0, 0.0)
        far_bias.append(rb_ref[T5_BUCKETS - 1, head])

    def step(hh, kb, bias, first):
        v_rows = slice(hh * HEAD_DIM, (hh + 1) * HEAD_DIM)
        s = _dot_nt(kb_scr[kb], qh_bf[hh])
        if first:
            s = s + bias
        else:
            s = s + (far_bias[hh] if bias is None else bias)
            s = jnp.where(sel_scr[hh, pl.ds(kb, 1), :] > 0.0, s, NEG_INF)
        m_new = jnp.max(s, axis=0, keepdims=True)
        if not first:
            m_new = jnp.maximum(m_scr[hh], m_new)
        p = jnp.exp(s - m_new)
        pv = _dot(vt_scr[kb, v_rows, :], p.astype(BF16))
        if first:
            l_scr[hh] = jnp.sum(p, axis=0, keepdims=True)
            acc_scr[v_rows, :] = pv
        else:
            alpha = jnp.exp(m_scr[hh] - m_new)
            l_scr[hh] = alpha * l_scr[hh] + jnp.sum(p, axis=0, keepdims=True)
            acc_scr[v_rows, :] = alpha * acc_scr[v_rows, :] + pv
        m_scr[hh] = m_new

    for hh in heads:
        step(hh, own, bias_scr[hh, 0], True)

    @pl.when(own >= 1)
    def _():
        for hh in heads:
            step(hh, own - 1, bias_scr[hh, 1], False)

    def far(kb, carry):
        for hh in heads:
            step(hh, kb, None, False)
        return carry

    lax.fori_loop(0, jnp.maximum(own - 1, 0), far, 0)
    for hh in heads:
        v_rows = slice(hh * HEAD_DIM, (hh + 1) * HEAD_DIM)
        acc_scr[v_rows, :] = acc_scr[v_rows, :] / l_scr[hh]
    o_ref[...] = acc_scr[...].T


def moba_prompt(q, k, v, rel_bias, n_b, seq):
    assert seq % MOBA_BLOCK == 0
    nq = seq // MOBA_BLOCK
    width = 2 * HEAD_DIM
    grid_spec = pltpu.PrefetchScalarGridSpec(
        num_scalar_prefetch=0,
        grid=(n_b, N_HEADS // 2, nq),
        in_specs=[pl.BlockSpec(memory_space=pltpu.SMEM),
                  pl.BlockSpec((MOBA_BLOCK, width), lambda n, p, t: (n * nq + t, p)),
                  pl.BlockSpec((seq, width), lambda n, p, t: (n, p)),
                  pl.BlockSpec((seq, width), lambda n, p, t: (n, p))],
        out_specs=pl.BlockSpec((MOBA_BLOCK, width), lambda n, p, t: (n * nq + t, p)),
        scratch_shapes=[pltpu.VMEM((nq, MOBA_BLOCK, width), BF16),
                        pltpu.VMEM((nq, width, MOBA_BLOCK), BF16),
                        pltpu.VMEM((nq, width), F32),
                        pltpu.VMEM((2, 2, MOBA_BLOCK, MOBA_BLOCK), F32),
                        pltpu.VMEM((2, nq, MOBA_BLOCK), F32),
                        pltpu.VMEM((2, 1, MOBA_BLOCK), F32),
                        pltpu.VMEM((2, 1, MOBA_BLOCK), F32),
                        pltpu.VMEM((width, MOBA_BLOCK), F32)],
    )
    return pl.pallas_call(
        _moba_prompt_kernel,
        grid_spec=grid_spec,
        out_shape=jax.ShapeDtypeStruct((n_b * seq, D_MODEL), F32),
        compiler_params=_params("parallel", "parallel", "arbitrary"),
        name="moba_prompt",
    )(rel_bias, q, k, v)


def _moba_sample_kernel(pt_ref, rbt_ref, q_ref, kn_ref, vn_ref, k0_ref, k1_ref, v0_ref, v1_ref,
                        o_ref, qbd_scr, gate_scr, m_scr, l_scr, acc_scr):
    b = pl.program_id(1)
    n_blk = pl.num_programs(1)
    blk = MOBA_BLOCK
    head_of_lane = lax.broadcasted_iota(jnp.int32, (N_HEADS, D_MODEL), 1) // HEAD_DIM
    head_of_row = lax.broadcasted_iota(jnp.int32, (N_HEADS, D_MODEL), 0)
    diag = head_of_lane == head_of_row

    @pl.when(b == 0)
    def _():
        qbd_scr[...] = jnp.where(diag, q_ref[0] * (HEAD_DIM ** -0.5), 0.0)

    qbd = qbd_scr[...]
    k_blk = jnp.concatenate([k0_ref[0], k1_ref[0]], axis=0)
    v_blk = jnp.concatenate([v0_ref[0], v1_ref[0]], axis=0)
    stat = (N_HEADS, STAT_LANES)
    gate = jnp.sum(qbd * jnp.mean(k_blk, axis=0, keepdims=True), axis=1, keepdims=True)
    gate_scr[b] = jnp.broadcast_to(gate, stat)
    s = _dot_nt(qbd.astype(BF16), k_blk.astype(BF16))
    dist = (n_blk - b) * blk - lax.broadcasted_iota(jnp.int32, (N_HEADS, blk), 1)
    s = s + _bias_from_dist(dist, lambda t: rbt_ref[:, t:t + 1])
    m = jnp.max(s, axis=1, keepdims=True)
    p = jnp.exp(s - m)
    m_scr[b] = jnp.broadcast_to(m, stat)
    l_scr[b] = jnp.broadcast_to(jnp.sum(p, axis=1, keepdims=True), stat)
    acc_scr[b] = _dot(p.astype(BF16), v_blk.astype(BF16))

    @pl.when(b == n_blk - 1)
    def _():
        n_past = m_scr.shape[0]
        gates = [gate_scr[i] for i in range(n_past)]
        s_self = jnp.sum(qbd * kn_ref[0], axis=1, keepdims=True) + rbt_ref[:, 0:1]
        m_tot = jnp.broadcast_to(s_self, stat)
        keep = []
        for i in range(n_past):
            rank = jnp.zeros(stat, F32)
            for i2 in range(n_past):
                if i2 == i:
                    continue
                ahead = (gates[i2] > gates[i]) if i2 > i else (gates[i2] >= gates[i])
                rank = rank + jnp.where(ahead, 1.0, 0.0)
            keep.append(rank < float(MOBA_TOPK))
            m_tot = jnp.where(keep[i], jnp.maximum(m_tot, m_scr[i]), m_tot)
        w_self = jnp.exp(s_self - m_tot)
        l_tot = w_self
        out = w_self[:, 0:1] * vn_ref[0]
        for i in range(n_past):
            w_i = jnp.where(keep[i], jnp.exp(m_scr[i] - m_tot), 0.0)
            l_tot = l_tot + w_i * l_scr[i]
            out = out + w_i[:, 0:1] * acc_scr[i]
        out = jnp.where(diag, out / l_tot[:, 0:1], 0.0)
        o_ref[0] = jnp.sum(out, axis=0, keepdims=True)


def moba_sample(q, k_new, v_new, cache_k, cache_v, page_table, rel_bias):
    n_d, n_pages = page_table.shape
    pages_per_blk = MOBA_BLOCK // PAGE_SIZE
    assert pages_per_blk == 2 and n_pages % pages_per_blk == 0
    n_blk = n_pages // pages_per_blk
    n_phys = cache_k.shape[0]
    ck = cache_k.reshape(n_phys, PAGE_SIZE, D_MODEL)
    cv = cache_v.reshape(n_phys, PAGE_SIZE, D_MODEL)
    row = pl.BlockSpec((1, 1, D_MODEL), lambda n, b, pt: (n, 0, 0))

    def page(which):
        return pl.BlockSpec((1, PAGE_SIZE, D_MODEL),
                            lambda n, b, pt: (pt[n, pages_per_blk * b + which], 0, 0))

    grid_spec = pltpu.PrefetchScalarGridSpec(
        num_scalar_prefetch=1,
        grid=(n_d, n_blk),
        in_specs=[pl.BlockSpec((N_HEADS, T5_BUCKETS), lambda n, b, pt: (0, 0)),
                  row, row, row, page(0), page(1), page(0), page(1)],
        out_specs=row,
        scratch_shapes=[pltpu.VMEM((N_HEADS, D_MODEL), F32),
                        pltpu.VMEM((n_blk, N_HEADS, STAT_LANES), F32),
                        pltpu.VMEM((n_blk, N_HEADS, STAT_LANES), F32),
                        pltpu.VMEM((n_blk, N_HEADS, STAT_LANES), F32),
                        pltpu.VMEM((n_blk, N_HEADS, D_MODEL), F32)],
    )
    out = pl.pallas_call(
        _moba_sample_kernel,
        grid_spec=grid_spec,
        out_shape=jax.ShapeDtypeStruct((n_d, 1, D_MODEL), F32),
        compiler_params=_params("parallel", "arbitrary"),
        name="moba_sample",
    )(page_table, rel_bias.T, q[:, None, :], k_new[:, None, :], v_new[:, None, :], ck, ck, cv, cv)
    return out[:, 0, :]


def kernel(x_prompt, x_sample, state_conv, cache_k, cache_v, page_table, p_prompt, p_sample, rel_bias, norm_mix_g, norm_ffn_g, norm_ple_g, norm_final_g, conv_w_in, conv_b_in, conv_dw_w, conv_dw_b, conv_ln_g, conv_ln_b, conv_w_out, conv_b_out, attn_w_qkv, attn_w_o, peer_w_q, peer_sub_keys, peer_u, peer_v, ple_w_proj, ple_w_gate):
    n_b, seq, d = x_prompt.shape
    n_d, dec_seq, _ = x_sample.shape
    assert d == D_MODEL and dec_seq == 1 and seq % TOKEN_TILE == 0
    depth = norm_mix_g.shape[0]
    n_p = n_b * seq
    tp = -(-(n_p + n_d) // TOKEN_TILE) * TOKEN_TILE
    pad = tp - n_p - n_d

    def pack(prompt_rows, sample_rows):
        tail = jnp.zeros((pad, prompt_rows.shape[1]), prompt_rows.dtype)
        return jnp.concatenate([prompt_rows, sample_rows.astype(prompt_rows.dtype), tail], axis=0)

    def row(vec):
        return vec.reshape(1, -1)

    h = pack(x_prompt.reshape(n_p, d), x_sample.reshape(n_d, d))
    conv_p, conv_s, kp, vp, ks, vs = [], [], [], [], [], []
    zero_bias = jnp.zeros((1, d), F32)
    for i in range(depth):
        j = i // 2
        if i % 2 == 0:
            glu = glu_in(h, row(norm_mix_g[i]), conv_w_in[j].astype(BF16), row(conv_b_in[j]))
            dwk = jnp.concatenate([conv_dw_w[j][::-1], jnp.zeros((1, d), F32)], axis=0)
            y_p, tail = conv_prompt(glu, n_b, seq, dwk, row(conv_dw_b[j]),
                                    row(conv_ln_g[j]), row(conv_ln_b[j]))
            glu_s = glu[n_p:n_p + n_d]
            y_s = conv_sample(glu_s, state_conv[j], conv_dw_w[j], row(conv_dw_b[j]),
                              row(conv_ln_g[j]), row(conv_ln_b[j]))
            conv_p.append(tail[:, HALO - (CONV_WIDTH - 1):, :])
            conv_s.append(jnp.concatenate([state_conv[j][:, 1:, :], glu_s[:, None, :]], axis=1))
            h = proj_residual(h, pack(y_p, y_s), conv_w_out[j].astype(BF16), row(conv_b_out[j]))
        else:
            wq_hi, wq_lo = _split(attn_w_qkv[j][:, :d])
            q, k, v = qkv_proj(h, row(norm_mix_g[i]), wq_hi, wq_lo, attn_w_qkv[j][:, d:].astype(BF16))
            o_p = moba_prompt(q, k, v, rel_bias, n_b, seq)
            o_s = moba_sample(q[n_p:n_p + n_d], k[n_p:n_p + n_d], v[n_p:n_p + n_d],
                              cache_k[j], cache_v[j], page_table, rel_bias)
            kp.append(k[:n_p].reshape(n_b, seq, N_HEADS, HEAD_DIM))
            vp.append(v[:n_p].reshape(n_b, seq, N_HEADS, HEAD_DIM))
            ks.append(k[n_p:n_p + n_d].reshape(n_d, 1, N_HEADS, HEAD_DIM))
            vs.append(v[n_p:n_p + n_d].reshape(n_d, 1, N_HEADS, HEAD_DIM))
            h = proj_residual(h, pack(o_p, o_s), attn_w_o[j].astype(BF16), zero_bias)
        h = peer_layer(h, row(norm_ffn_g[i]), peer_w_q[i], peer_sub_keys[i], peer_u[i], peer_v[i])
        p_all = pack(p_prompt[i].reshape(n_p, -1), p_sample[i].reshape(n_d, -1))
        h = ple(h, p_all, row(norm_ple_g[i]), ple_w_gate[i].astype(BF16),
                ple_w_proj[i].astype(BF16), row(norm_final_g), final_norm=(i == depth - 1))
    y_prompt = h[:n_p].reshape(n_b, seq, d)
    y_sample = h[n_p:n_p + n_d].reshape(n_d, 1, d)
    return (y_prompt, y_sample, jnp.stack(conv_p), jnp.stack(conv_s),
            jnp.stack(kp), jnp.stack(vp), jnp.stack(ks), jnp.stack(vs))
```

```python
import functools
import math

import numpy as np
import jax
import jax.numpy as jnp
from jax import lax
from jax.experimental import pallas as pl
from jax.experimental.pallas import tpu as pltpu

F32 = jnp.float32
BF16 = jnp.bfloat16
EPS = 1e-6
NEG_INF = float("-inf")

D_MODEL = 1024
N_HEADS = 16
HEAD_DIM = 64
CONV_WIDTH = 31
MOBA_BLOCK = 256
MOBA_TOPK = 3
T5_BUCKETS = 32
T5_MAX_DIST = 128
PEER_HEADS = 8
PEER_N_KEYS = 128
PEER_KEY_HALF = 64
PEER_TOPK = 16
PAGE_SIZE = 128

TOKEN_TILE = 512
ROUTE_TILE = 256
ROUTE_HEADS = 2
EXPERT_TILE = 2048
EXPERT_CHUNK = 2048
CONV_CHUNK = 64
HALO = 32
STAT_LANES = 128
LANES = 128
BF16_ROWS = 16
VMEM_LIMIT = 56 * 1024 * 1024

CAND_COUNT = [PEER_TOPK // (j + 1) for j in range(PEER_TOPK)]
CAND_ROWS = [16] + [8] * (PEER_TOPK - 1)
CAND_TOTAL = sum(CAND_ROWS)


def _params(*sem):
    return pltpu.CompilerParams(dimension_semantics=sem, vmem_limit_bytes=VMEM_LIMIT)


def _dot(a, b):
    return jnp.dot(a, b, preferred_element_type=F32)


def _dot_nt(a, b):
    return lax.dot_general(a, b, (((1,), (1,)), ((), ())), preferred_element_type=F32)


def _split(x):
    hi = x.astype(BF16)
    lo = (x - hi.astype(F32)).astype(BF16)
    return hi, lo


def _dot3(a_hi, a_lo, b_hi, b_lo):
    return _dot(a_hi, b_hi) + _dot(a_hi, b_lo) + _dot(a_lo, b_hi)


def _rms(x, g):
    return x * lax.rsqrt(jnp.mean(x * x, axis=-1, keepdims=True) + EPS) * g


def _sigmoid(x):
    return 1.0 / (1.0 + jnp.exp(-x))


def _row_spec(tile, width):
    return pl.BlockSpec((tile, width), lambda i: (i, 0))


def _full_spec(shape):
    nd = len(shape)
    return pl.BlockSpec(shape, lambda *_: (0,) * nd)


def _glu_in_kernel(h_ref, g_ref, w_ref, b_ref, o_ref):
    a = _rms(h_ref[...], g_ref[...]).astype(BF16)
    z = _dot(a, w_ref[...]) + b_ref[...]
    o_ref[...] = z[:, :D_MODEL] * _sigmoid(z[:, D_MODEL:])


def glu_in(h, g, w_bf, b):
    tp = h.shape[0]
    return pl.pallas_call(
        _glu_in_kernel,
        grid=(tp // TOKEN_TILE,),
        in_specs=[_row_spec(TOKEN_TILE, D_MODEL), _full_spec((1, D_MODEL)),
                  _full_spec((D_MODEL, 2 * D_MODEL)), _full_spec((1, 2 * D_MODEL))],
        out_specs=_row_spec(TOKEN_TILE, D_MODEL),
        out_shape=jax.ShapeDtypeStruct((tp, D_MODEL), F32),
        compiler_params=_params("parallel"),
        name="glu_in",
    )(h, g, w_bf, b)


def _proj_residual_kernel(h_ref, y_ref, w_ref, b_ref, o_ref):
    o_ref[...] = h_ref[...] + _dot(y_ref[...].astype(BF16), w_ref[...]) + b_ref[...]


def proj_residual(h, y, w_bf, b):
    tp = h.shape[0]
    return pl.pallas_call(
        _proj_residual_kernel,
        grid=(tp // TOKEN_TILE,),
        in_specs=[_row_spec(TOKEN_TILE, D_MODEL), _row_spec(TOKEN_TILE, D_MODEL),
                  _full_spec((D_MODEL, D_MODEL)), _full_spec((1, D_MODEL))],
        out_specs=_row_spec(TOKEN_TILE, D_MODEL),
        out_shape=jax.ShapeDtypeStruct((tp, D_MODEL), F32),
        compiler_params=_params("parallel"),
        name="proj_residual",
    )(h, y, w_bf, b)


def _ple_kernel(h_ref, p_ref, g_ref, wg_ref, wp_ref, gf_ref, o_ref, *, final_norm):
    h = h_ref[...]
    gate = _sigmoid(_dot(_rms(h, g_ref[...]).astype(BF16), wg_ref[...]))
    out = h + gate * _dot(p_ref[...].astype(BF16), wp_ref[...])
    if final_norm:
        out = _rms(out, gf_ref[...])
    o_ref[...] = out


def ple(h, p, g, wg_bf, wp_bf, g_final, final_norm):
    tp = h.shape[0]
    pdim = p.shape[1]
    return pl.pallas_call(
        functools.partial(_ple_kernel, final_norm=final_norm),
        grid=(tp // TOKEN_TILE,),
        in_specs=[_row_spec(TOKEN_TILE, D_MODEL), _row_spec(TOKEN_TILE, pdim),
                  _full_spec((1, D_MODEL)), _full_spec((D_MODEL, D_MODEL)),
                  _full_spec((pdim, D_MODEL)), _full_spec((1, D_MODEL))],
        out_specs=_row_spec(TOKEN_TILE, D_MODEL),
        out_shape=jax.ShapeDtypeStruct((tp, D_MODEL), F32),
        compiler_params=_params("parallel"),
        name="ple_final" if final_norm else "ple",
    )(h, p, g, wg_bf, wp_bf, g_final)


def _qkv_kernel(h_ref, g_ref, wq_hi_ref, wq_lo_ref, wkv_ref, q_ref, k_ref, v_ref):
    a_hi, a_lo = _split(_rms(h_ref[...], g_ref[...]))
    q_ref[...] = _dot3(a_hi, a_lo, wq_hi_ref[...], wq_lo_ref[...])
    kv = _dot(a_hi, wkv_ref[...])
    k_ref[...] = kv[:, :D_MODEL]
    v_ref[...] = kv[:, D_MODEL:]


def qkv_proj(h, g, wq_hi, wq_lo, wkv_bf):
    tp = h.shape[0]
    out = jax.ShapeDtypeStruct((tp, D_MODEL), F32)
    return pl.pallas_call(
        _qkv_kernel,
        grid=(tp // TOKEN_TILE,),
        in_specs=[_row_spec(TOKEN_TILE, D_MODEL), _full_spec((1, D_MODEL)),
                  _full_spec((D_MODEL, D_MODEL)), _full_spec((D_MODEL, D_MODEL)),
                  _full_spec((D_MODEL, 2 * D_MODEL))],
        out_specs=[_row_spec(TOKEN_TILE, D_MODEL)] * 3,
        out_shape=[out, out, out],
        compiler_params=_params("parallel"),
        name="qkv_proj",
    )(h, g, wq_hi, wq_lo, wkv_bf)


def _ln_silu(y, g, b):
    mu = jnp.mean(y, axis=-1, keepdims=True)
    yc = y - mu
    y = yc * lax.rsqrt(jnp.mean(yc * yc, axis=-1, keepdims=True) + EPS) * g + b
    return y * _sigmoid(y)


def _conv_prompt_kernel(x_ref, dwk_ref, dwb_ref, lng_ref, lnb_ref, y_ref, st_ref, buf_ref):
    tile = x_ref.shape[0]

    @pl.when(pl.program_id(1) == 0)
    def _():
        buf_ref[0:HALO, :] = jnp.zeros((HALO, D_MODEL), F32)

    buf_ref[HALO:HALO + tile, :] = x_ref[...]

    def chunk(c, carry):
        t0 = pl.multiple_of(c * CONV_CHUNK, CONV_CHUNK)
        win = buf_ref[pl.ds(t0, CONV_CHUNK + HALO), :]
        acc = None
        for s in range(8):
            part = None
            for u in range(4):
                k = 8 * u + s
                term = win[24 - 8 * u:24 - 8 * u + CONV_CHUNK + 8, :] * dwk_ref[k:k + 1, :]
                part = term if part is None else part + term
            if s:
                part = pltpu.roll(part, s, 0)
            part = part[8:8 + CONV_CHUNK, :]
            acc = part if acc is None else acc + part
        y = _ln_silu(acc + dwb_ref[...], lng_ref[...], lnb_ref[...])
        y_ref[pl.ds(t0, CONV_CHUNK), :] = y.astype(y_ref.dtype)
        return carry

    lax.fori_loop(0, tile // CONV_CHUNK, chunk, 0)
    tail = buf_ref[tile:tile + HALO, :]
    st_ref[0] = tail
    buf_ref[0:HALO, :] = tail


def conv_prompt(glu, n_b, seq, dwk, dwb, lng, lnb):
    nt = seq // TOKEN_TILE
    return pl.pallas_call(
        _conv_prompt_kernel,
        grid=(n_b, nt),
        in_specs=[pl.BlockSpec((TOKEN_TILE, D_MODEL), lambda n, t: (n * nt + t, 0)),
                  _full_spec((32, D_MODEL)), _full_spec((1, D_MODEL)),
                  _full_spec((1, D_MODEL)), _full_spec((1, D_MODEL))],
        out_specs=[pl.BlockSpec((TOKEN_TILE, D_MODEL), lambda n, t: (n * nt + t, 0)),
                   pl.BlockSpec((1, HALO, D_MODEL), lambda n, t: (n, 0, 0))],
        out_shape=[jax.ShapeDtypeStruct((n_b * seq, D_MODEL), BF16),
                   jax.ShapeDtypeStruct((n_b, HALO, D_MODEL), F32)],
        scratch_shapes=[pltpu.VMEM((TOKEN_TILE + HALO, D_MODEL), F32)],
        compiler_params=_params("arbitrary", "arbitrary"),
        name="conv_prompt",
    )(glu, dwk, dwb, lng, lnb)


def _conv_sample_kernel(x_ref, st_ref, dw_ref, dwb_ref, lng_ref, lnb_ref, y_ref):
    hist = jnp.sum(st_ref[...] * dw_ref[0:CONV_WIDTH - 1, :][None], axis=1)
    y = hist + x_ref[...] * dw_ref[CONV_WIDTH - 1:CONV_WIDTH, :] + dwb_ref[...]
    y_ref[...] = _ln_silu(y, lng_ref[...], lnb_ref[...]).astype(y_ref.dtype)


def conv_sample(glu_s, state, dw, dwb, lng, lnb):
    n_d = glu_s.shape[0]
    blk = 32 if n_d % 32 == 0 else n_d
    return pl.pallas_call(
        _conv_sample_kernel,
        grid=(n_d // blk,),
        in_specs=[_row_spec(blk, D_MODEL),
                  pl.BlockSpec((blk, CONV_WIDTH - 1, D_MODEL), lambda i: (i, 0, 0)),
                  _full_spec((CONV_WIDTH, D_MODEL)), _full_spec((1, D_MODEL)),
                  _full_spec((1, D_MODEL)), _full_spec((1, D_MODEL))],
        out_specs=_row_spec(blk, D_MODEL),
        out_shape=jax.ShapeDtypeStruct((n_d, D_MODEL), BF16),
        compiler_params=_params("parallel"),
        name="conv_sample",
    )(glu_s, state, dw, dwb, lng, lnb)


def _extract_top(w, tie_key, vals_ref, first_ref):
    def body(r, w):
        m = jnp.max(w, axis=1, keepdims=True)
        first = jnp.min(jnp.where(w == m, tie_key, 1e9), axis=1, keepdims=True)
        for g in range(w.shape[0]):
            if vals_ref is not None:
                vals_ref[g, pl.ds(r, 1), :] = m[g]
            if first_ref is not None:
                first_ref[g, pl.ds(r, 1), :] = first[g]
        return jnp.where(tie_key == first, NEG_INF, w)

    return lax.fori_loop(0, PEER_TOPK, body, w)


def _peer_route_kernel(x_ref, g_ref, wq_hi_ref, wq_lo_ref, keys_ref, cpos_ref, cneg_ref,
                       xt_ref, rank2_ref, e2_ref, nsel_ref, c1_ref,
                       q_scr, s_scr, v_scr, i_scr, c_scr):
    tile = x_ref.shape[0]
    xn_t = _rms(x_ref[...], g_ref[...]).T
    x_hi, x_lo = _split(xn_t)
    xt_ref[...] = x_hi
    q_scr[...] = (_dot(wq_hi_ref[...], x_hi) + _dot(wq_hi_ref[...], x_lo)
                  + _dot(wq_lo_ref[...], x_hi))
    key_iota = lax.broadcasted_iota(jnp.int32, (PEER_N_KEYS, tile), 0).astype(F32)
    cpos = cpos_ref[...]

    def head_group(hg, carry):
        for g in range(2 * ROUTE_HEADS):
            i = 2 * ROUTE_HEADS * hg + g
            qs = q_scr[pl.ds(pl.multiple_of(i * PEER_KEY_HALF, PEER_KEY_HALF), PEER_KEY_HALF), :]
            s_scr[g] = jnp.dot(keys_ref[i], qs, precision=lax.Precision.HIGHEST,
                               preferred_element_type=F32)
        _extract_top(s_scr[...], key_iota, v_scr, i_scr)
        for hh in range(ROUTE_HEADS):
            v2_16 = v_scr[2 * hh + 1]
            v2_8 = v_scr[2 * hh + 1, 0:8, :]
            c_scr[hh] = jnp.concatenate(
                [v_scr[2 * hh, j:j + 1, :] + (v2_16 if j == 0 else v2_8) for j in range(PEER_TOPK)],
                axis=0) + cneg_ref[...]
        left = _extract_top(c_scr[...], cpos, None, None)
        for hh in range(ROUTE_HEADS):
            h = ROUTE_HEADS * hg + hh
            max1 = v_scr[2 * hh, 0:1, :]
            max2 = v_scr[2 * hh + 1, 0:1, :]
            cand = c_scr[hh]
            picked = jnp.where((left[hh] == NEG_INF) & (cand > NEG_INF), 1.0, 0.0)
            z = jnp.sum(picked * jnp.exp(cand - (max1 + max2)), axis=0, keepdims=True)
            nsel = jnp.zeros((PEER_N_KEYS, tile), F32)
            rank2 = jnp.full((PEER_N_KEYS, tile), float(PEER_TOPK), F32)
            row = 0
            for j in range(PEER_TOPK):
                n_j = jnp.sum(picked[row:row + CAND_ROWS[j], :], axis=0, keepdims=True)
                nsel = jnp.where(key_iota == i_scr[2 * hh, j:j + 1, :], n_j, nsel)
                rank2 = jnp.where(key_iota == i_scr[2 * hh + 1, j:j + 1, :], float(j), rank2)
                row += CAND_ROWS[j]
            rank2_ref[h] = rank2.astype(rank2_ref.dtype)
            nsel_ref[h] = nsel
            e2_ref[h] = jnp.exp(s_scr[2 * hh + 1] - max2).astype(e2_ref.dtype)
            c1_ref[h] = jnp.exp(s_scr[2 * hh] - max1) / z
        return carry

    lax.fori_loop(0, PEER_HEADS // ROUTE_HEADS, head_group, 0)


def _cand_constants(tile):
    pos, neg = [], []
    for j in range(PEER_TOPK):
        for l in range(CAND_ROWS[j]):
            pos.append(16.0 * j + l)
            neg.append(0.0 if l < CAND_COUNT[j] else NEG_INF)
    pos = np.broadcast_to(np.asarray(pos, np.float32)[:, None], (CAND_TOTAL, tile))
    neg = np.broadcast_to(np.asarray(neg, np.float32)[:, None], (CAND_TOTAL, tile))
    return jnp.asarray(pos), jnp.asarray(neg)


def peer_route(h, g, wq_t_hi, wq_t_lo, keys):
    tp = h.shape[0]
    cpos, cneg = _cand_constants(ROUTE_TILE)
    dense = jax.ShapeDtypeStruct((PEER_HEADS, PEER_N_KEYS, tp), F32)
    dense_bf = jax.ShapeDtypeStruct((PEER_HEADS, PEER_N_KEYS, tp), BF16)
    dense_spec = pl.BlockSpec((PEER_HEADS, PEER_N_KEYS, ROUTE_TILE), lambda i: (0, 0, i))
    return pl.pallas_call(
        _peer_route_kernel,
        grid=(tp // ROUTE_TILE,),
        in_specs=[_row_spec(ROUTE_TILE, D_MODEL), _full_spec((1, D_MODEL)),
                  _full_spec((D_MODEL, D_MODEL)), _full_spec((D_MODEL, D_MODEL)),
                  _full_spec((2 * PEER_HEADS, PEER_N_KEYS, PEER_KEY_HALF)),
                  _full_spec((CAND_TOTAL, ROUTE_TILE)), _full_spec((CAND_TOTAL, ROUTE_TILE))],
        out_specs=[pl.BlockSpec((D_MODEL, ROUTE_TILE), lambda i: (0, i)),
                   dense_spec, dense_spec, dense_spec, dense_spec],
        out_shape=[jax.ShapeDtypeStruct((D_MODEL, tp), BF16), dense_bf, dense_bf, dense, dense],
        scratch_shapes=[pltpu.VMEM((D_MODEL, ROUTE_TILE), F32),
                        pltpu.VMEM((2 * ROUTE_HEADS, PEER_N_KEYS, ROUTE_TILE), F32),
                        pltpu.VMEM((2 * ROUTE_HEADS, PEER_TOPK, ROUTE_TILE), F32),
                        pltpu.VMEM((2 * ROUTE_HEADS, PEER_TOPK, ROUTE_TILE), F32),
                        pltpu.VMEM((ROUTE_HEADS, CAND_TOTAL, ROUTE_TILE), F32)],
        compiler_params=_params("parallel"),
        name="peer_route",
    )(h, g, wq_t_hi, wq_t_lo, keys, cpos, cneg)


def _peer_dense_kernel(xt_ref, u_ref, v_ref, rank2_in, e2_in, nsel_ref, c1_ref, h_ref,
                       o_ref, act_scr, g_scr, w_scr, acc_scr, rank2_ref, e2_ref):
    j = pl.program_id(1)

    @pl.when(j == 0)
    def _():
        acc_scr[...] = jnp.zeros(acc_scr.shape, F32)
        rank2_ref[...] = rank2_in[...]
        e2_ref[...] = e2_in[...]

    pack = BF16_ROWS
    zero = jnp.zeros((1, LANES), BF16)
    n_cols = xt_ref.shape[1] // LANES
    a_per_chunk = EXPERT_CHUNK // PEER_N_KEYS
    for chunk in range(EXPERT_TILE // EXPERT_CHUNK):
        experts = slice(chunk * EXPERT_CHUNK, (chunk + 1) * EXPERT_CHUNK)
        for a_local in range(chunk * a_per_chunk, (chunk + 1) * a_per_chunk):
            a_row = slice(a_local, a_local + 1)
            rows = slice(a_local * PEER_N_KEYS, (a_local + 1) * PEER_N_KEYS)
            for col in range(n_cols):
                cols = slice(col * LANES, (col + 1) * LANES)
                gate = None
                for h in range(PEER_HEADS):
                    n_row = jnp.broadcast_to(nsel_ref[h, a_row, cols], (pack, LANES)).astype(BF16)
                    c_row = jnp.broadcast_to(c1_ref[h, a_row, cols], (pack, LANES)).astype(BF16)
                    n_all = jnp.concatenate([n_row] * (PEER_N_KEYS // pack), axis=0)
                    c_all = jnp.concatenate([c_row] * (PEER_N_KEYS // pack), axis=0)
                    term = jnp.where(rank2_ref[h, :, cols] < n_all, e2_ref[h, :, cols] * c_all, zero)
                    gate = term if gate is None else gate + term
                g_scr[rows, cols] = gate
        act_scr[experts, :] = _dot(u_ref[experts, :], xt_ref[...])
        for a_local in range(chunk * a_per_chunk, (chunk + 1) * a_per_chunk):
            rows = slice(a_local * PEER_N_KEYS, (a_local + 1) * PEER_N_KEYS)
            for col in range(n_cols):
                cols = slice(col * LANES, (col + 1) * LANES)
                act = act_scr[rows, cols]
                gelu = 0.5 * act * (1.0 + lax.erf(act * (1.0 / math.sqrt(2.0))))
                w_scr[rows, cols] = gelu.astype(BF16) * g_scr[rows, cols]
        acc_scr[...] += _dot(v_ref[:, experts], w_scr[experts, :])

    @pl.when(j == pl.num_programs(1) - 1)
    def _():
        o_ref[...] = h_ref[...] + acc_scr[...].T


def peer_dense(h, xt, u_bf, v_bf, rank2, e2, nsel, c1):
    tp = h.shape[0]
    n_exp = u_bf.shape[0]
    dense_spec = pl.BlockSpec((PEER_HEADS, PEER_N_KEYS, TOKEN_TILE), lambda i, j: (0, 0, i))
    a_spec = pl.BlockSpec((PEER_HEADS, EXPERT_TILE // PEER_N_KEYS, TOKEN_TILE),
                          lambda i, j: (0, j, i))
    return pl.pallas_call(
        _peer_dense_kernel,
        grid=(tp // TOKEN_TILE, n_exp // EXPERT_TILE),
        in_specs=[pl.BlockSpec((D_MODEL, TOKEN_TILE), lambda i, j: (0, i)),
                  pl.BlockSpec((EXPERT_TILE, D_MODEL), lambda i, j: (j, 0)),
                  pl.BlockSpec((D_MODEL, EXPERT_TILE), lambda i, j: (0, j)),
                  dense_spec, dense_spec, a_spec, a_spec,
                  pl.BlockSpec((TOKEN_TILE, D_MODEL), lambda i, j: (i, 0))],
        out_specs=pl.BlockSpec((TOKEN_TILE, D_MODEL), lambda i, j: (i, 0)),
        out_shape=jax.ShapeDtypeStruct((tp, D_MODEL), F32),
        scratch_shapes=[pltpu.VMEM((EXPERT_TILE, TOKEN_TILE), F32),
                        pltpu.VMEM((EXPERT_TILE, TOKEN_TILE), BF16),
                        pltpu.VMEM((EXPERT_TILE, TOKEN_TILE), BF16),
                        pltpu.VMEM((D_MODEL, TOKEN_TILE), F32),
                        pltpu.VMEM((PEER_HEADS, PEER_N_KEYS, TOKEN_TILE), BF16),
                        pltpu.VMEM((PEER_HEADS, PEER_N_KEYS, TOKEN_TILE), BF16)],
        compiler_params=_params("parallel", "arbitrary"),
        name="peer_dense",
    )(xt, u_bf, v_bf, rank2, e2, nsel, c1, h)


def peer_layer(h, g, wq, sub_keys, u, v):
    wq_t_hi, wq_t_lo = _split(wq.T)
    keys = sub_keys.reshape(2 * PEER_HEADS, PEER_N_KEYS, PEER_KEY_HALF)
    xt, rank2, e2, nsel, c1 = peer_route(h, g, wq_t_hi, wq_t_lo, keys)
    return peer_dense(h, xt, u.astype(BF16), v.astype(BF16).T, rank2, e2, nsel, c1)


def _bucket_starts():
    max_exact = T5_BUCKETS // 2
    d = np.arange(0, 4 * T5_MAX_DIST)
    df = np.maximum(d, 1).astype(np.float64)
    large = max_exact + (np.log(df / max_exact) / math.log(T5_MAX_DIST / max_exact)
                         * (T5_BUCKETS - max_exact)).astype(np.int64)
    bucket = np.where(d < max_exact, d, np.minimum(large, T5_BUCKETS - 1))
    starts = []
    for b in range(T5_BUCKETS):
        idx = np.nonzero(bucket == b)[0]
        if idx.size:
            starts.append((int(idx[0]), b))
    return starts


BUCKET_STARTS = _bucket_starts()


def _bias_from_dist(dist, table):
    out = None
    for start, b in BUCKET_STARTS:
        val = table(b)
        out = val + jnp.zeros(dist.shape, F32) if out is None else jnp.where(dist >= start, val, out)
    return out


def _moba_prompt_kernel(rb_ref, q_ref, k_ref, v_ref, o_ref,
                        kb_scr, vt_scr, km_scr, bias_scr, sel_scr, m_scr, l_scr, acc_scr):
    pair = pl.program_id(1)
    qt = pl.program_id(2)
    n_blk = k_ref.shape[0] // MOBA_BLOCK
    blk = MOBA_BLOCK
    lane = lax.broadcasted_iota(jnp.int32, (1, 2 * HEAD_DIM), 1)

    @pl.when(qt == 0)
    def _():
        for b in range(n_blk):
            kt = k_ref[b * blk:(b + 1) * blk, :]
            kb_scr[b] = kt.astype(BF16)
            km_scr[b:b + 1, :] = jnp.mean(kt, axis=0, keepdims=True)
            vt_scr[b] = v_ref[b * blk:(b + 1) * blk, :].T.astype(BF16)
        ik = lax.broadcasted_iota(jnp.int32, (blk, blk), 0)
        iq = lax.broadcasted_iota(jnp.int32, (blk, blk), 1)
        for hh in range(2):
            head = 2 * pair + hh
            for delta in range(2):
                dist = iq - ik + delta * blk
                bias = _bias_from_dist(jnp.maximum(dist, 0), lambda b: rb_ref[b, head])
                bias_scr[hh, delta] = jnp.where(dist >= 0, bias, NEG_INF)

    own = qt
    blk_iota = lax.broadcasted_iota(jnp.int32, (n_blk, blk), 0)
    heads = range(2)
    qh_bf, far_bias = [], []
    for hh in heads:
        head = 2 * pair + hh
        in_head = (lane >= hh * HEAD_DIM) & (lane < (hh + 1) * HEAD_DIM)
        qh = jnp.where(in_head, q_ref[...] * (HEAD_DIM ** -0.5), 0.0)
        qh_bf.append(qh.astype(BF16))
        km = jnp.where(in_head, km_scr[...], 0.0)
        gate = lax.dot_general(km, qh, (((1,), (1,)), ((), ())),
                               precision=lax.Precision.HIGHEST,
                               preferred_element_type=F32)
        gate = jnp.where(blk_iota < own, gate, NEG_INF)
        rank = jnp.zeros((n_blk, blk), F32)
        for b in range(n_blk):
            gb = gate[b:b + 1, :]
            ahead = (gb > gate) | ((gb == gate) & (b < blk_iota))
            rank = rank + jnp.where(ahead, 1.0, 0.0)
        sel_scr[hh] = jnp.where((blk_iota < own) & (rank < float(MOBA_TOPK)), 1.0, 0.0)
        far_bias.append(rb_ref[T5_BUCKETS - 1, head])

    def step(hh, blocks, first):
        v_rows = slice(hh * HEAD_DIM, (hh + 1) * HEAD_DIM)
        scores = []
        for kb, bias in blocks:
            s = _dot_nt(kb_scr[kb], qh_bf[hh])
            if first:
                s = s + bias
            else:
                s = s + (far_bias[hh] if bias is None else bias)
                s = jnp.where(sel_scr[hh, pl.ds(kb, 1), :] > 0.0, s, NEG_INF)
            scores.append(s)
        m_new = None if first else m_scr[hh]
        for s in scores:
            m_blk = jnp.max(s, axis=0, keepdims=True)
            m_new = m_blk if m_new is None else jnp.maximum(m_new, m_blk)
        pv, p_sum = None, None
        for (kb, _), s in zip(blocks, scores):
            p = jnp.exp(s - m_new)
            part = _dot(vt_scr[kb, v_rows, :], p.astype(BF16))
            part_sum = jnp.sum(p, axis=0, keepdims=True)
            pv = part if pv is None else pv + part
            p_sum = part_sum if p_sum is None else p_sum + part_sum
        if first:
            l_scr[hh] = p_sum
            acc_scr[v_rows, :] = pv
        else:
            alpha = jnp.exp(m_scr[hh] - m_new)
            l_scr[hh] = alpha * l_scr[hh] + p_sum
            acc_scr[v_rows, :] = alpha * acc_scr[v_rows, :] + pv
        m_scr[hh] = m_new

    for hh in heads:
        step(hh, [(own, bias_scr[hh, 0])], True)

    @pl.when(own >= 1)
    def _():
        for hh in heads:
            step(hh, [(own - 1, bias_scr[hh, 1])], False)

    n_far = jnp.maximum(own - 1, 0)

    def far_pair(i, carry):
        for hh in heads:
            step(hh, [(2 * i, None), (2 * i + 1, None)], False)
        return carry

    lax.fori_loop(0, n_far // 2, far_pair, 0)

    @pl.when(n_far % 2 == 1)
    def _():
        for hh in heads:
            step(hh, [(n_far - 1, None)], False)
    for hh in heads:
        v_rows = slice(hh * HEAD_DIM, (hh + 1) * HEAD_DIM)
        acc_scr[v_rows, :] = acc_scr[v_rows, :] / l_scr[hh]
    o_ref[...] = acc_scr[...].T


def moba_prompt(q, k, v, rel_bias, n_b, seq):
    assert seq % MOBA_BLOCK == 0
    nq = seq // MOBA_BLOCK
    width = 2 * HEAD_DIM
    grid_spec = pltpu.PrefetchScalarGridSpec(
        num_scalar_prefetch=0,
        grid=(n_b, N_HEADS // 2, nq),
        in_specs=[pl.BlockSpec(memory_space=pltpu.SMEM),
                  pl.BlockSpec((MOBA_BLOCK, width), lambda n, p, t: (n * nq + t, p)),
                  pl.BlockSpec((seq, width), lambda n, p, t: (n, p)),
                  pl.BlockSpec((seq, width), lambda n, p, t: (n, p))],
        out_specs=pl.BlockSpec((MOBA_BLOCK, width), lambda n, p, t: (n * nq + t, p)),
        scratch_shapes=[pltpu.VMEM((nq, MOBA_BLOCK, width), BF16),
                        pltpu.VMEM((nq, width, MOBA_BLOCK), BF16),
                        pltpu.VMEM((nq, width), F32),
                        pltpu.VMEM((2, 2, MOBA_BLOCK, MOBA_BLOCK), F32),
                        pltpu.VMEM((2, nq, MOBA_BLOCK), F32),
                        pltpu.VMEM((2, 1, MOBA_BLOCK), F32),
                        pltpu.VMEM((2, 1, MOBA_BLOCK), F32),
                        pltpu.VMEM((width, MOBA_BLOCK), F32)],
    )
    return pl.pallas_call(
        _moba_prompt_kernel,
        grid_spec=grid_spec,
        out_shape=jax.ShapeDtypeStruct((n_b * seq, D_MODEL), F32),
        compiler_params=_params("parallel", "parallel", "arbitrary"),
        name="moba_prompt",
    )(rel_bias, q, k, v)


def _moba_sample_kernel(pt_ref, rbt_ref, q_ref, kn_ref, vn_ref, *refs, n_pages):
    k_refs, v_refs, o_ref = refs[:n_pages], refs[n_pages:2 * n_pages], refs[2 * n_pages]
    blk = MOBA_BLOCK
    pages_per_blk = blk // PAGE_SIZE
    n_blk = n_pages // pages_per_blk
    stat = (N_HEADS, STAT_LANES)
    head_of_lane = lax.broadcasted_iota(jnp.int32, (N_HEADS, D_MODEL), 1) // HEAD_DIM
    head_of_row = lax.broadcasted_iota(jnp.int32, (N_HEADS, D_MODEL), 0)
    diag = head_of_lane == head_of_row
    qbd = jnp.where(diag, q_ref[0] * (HEAD_DIM ** -0.5), 0.0)
    qbd_bf = qbd.astype(BF16)
    key_iota = lax.broadcasted_iota(jnp.int32, (N_HEADS, blk), 1)

    gates, maxes, sums, outs = [], [], [], []
    for b in range(n_blk):
        pages = range(pages_per_blk * b, pages_per_blk * (b + 1))
        k_blk = jnp.concatenate([k_refs[i][0] for i in pages], axis=0)
        v_blk = jnp.concatenate([v_refs[i][0] for i in pages], axis=0)
        gate = jnp.sum(qbd * jnp.mean(k_blk, axis=0, keepdims=True), axis=1, keepdims=True)
        s = _dot_nt(qbd_bf, k_blk.astype(BF16))
        dist = (n_blk - b) * blk - key_iota
        s = s + _bias_from_dist(dist, lambda t: rbt_ref[:, t:t + 1])
        m = jnp.max(s, axis=1, keepdims=True)
        p = jnp.exp(s - m)
        gates.append(jnp.broadcast_to(gate, stat))
        maxes.append(jnp.broadcast_to(m, stat))
        sums.append(jnp.broadcast_to(jnp.sum(p, axis=1, keepdims=True), stat))
        outs.append(_dot(p.astype(BF16), v_blk.astype(BF16)))

    s_self = jnp.sum(qbd * kn_ref[0], axis=1, keepdims=True) + rbt_ref[:, 0:1]
    m_tot = jnp.broadcast_to(s_self, stat)
    keep = []
    for i in range(n_blk):
        rank = jnp.zeros(stat, F32)
        for i2 in range(n_blk):
            if i2 == i:
                continue
            ahead = (gates[i2] > gates[i]) if i2 > i else (gates[i2] >= gates[i])
            rank = rank + jnp.where(ahead, 1.0, 0.0)
        keep.append(rank < float(MOBA_TOPK))
        m_tot = jnp.where(keep[i], jnp.maximum(m_tot, maxes[i]), m_tot)
    w_self = jnp.exp(s_self - m_tot)
    l_tot = w_self
    out = w_self[:, 0:1] * vn_ref[0]
    for i in range(n_blk):
        w_i = jnp.where(keep[i], jnp.exp(maxes[i] - m_tot), 0.0)
        l_tot = l_tot + w_i * sums[i]
        out = out + w_i[:, 0:1] * outs[i]
    out = jnp.where(diag, out / l_tot[:, 0:1], 0.0)
    o_ref[0] = jnp.sum(out, axis=0, keepdims=True)


def moba_sample(q, k_new, v_new, cache_k, cache_v, page_table, rel_bias):
    n_d, n_pages = page_table.shape
    assert n_pages % (MOBA_BLOCK // PAGE_SIZE) == 0
    n_phys = cache_k.shape[0]
    ck = cache_k.reshape(n_phys, PAGE_SIZE, D_MODEL)
    cv = cache_v.reshape(n_phys, PAGE_SIZE, D_MODEL)
    row = pl.BlockSpec((1, 1, D_MODEL), lambda n, pt: (n, 0, 0))

    def page(i):
        return pl.BlockSpec((1, PAGE_SIZE, D_MODEL), lambda n, pt: (pt[n, i], 0, 0))

    pages = [page(i) for i in range(n_pages)]
    grid_spec = pltpu.PrefetchScalarGridSpec(
        num_scalar_prefetch=1,
        grid=(n_d,),
        in_specs=[pl.BlockSpec((N_HEADS, T5_BUCKETS), lambda n, pt: (0, 0)),
                  row, row, row] + pages + pages,
        out_specs=row,
    )
    out = pl.pallas_call(
        functools.partial(_moba_sample_kernel, n_pages=n_pages),
        grid_spec=grid_spec,
        out_shape=jax.ShapeDtypeStruct((n_d, 1, D_MODEL), F32),
        compiler_params=_params("parallel"),
        name="moba_sample",
    )(page_table, rel_bias.T, q[:, None, :], k_new[:, None, :], v_new[:, None, :],
      *([ck] * n_pages), *([cv] * n_pages))
    return out[:, 0, :]


def kernel(x_prompt, x_sample, state_conv, cache_k, cache_v, page_table, p_prompt, p_sample, rel_bias, norm_mix_g, norm_ffn_g, norm_ple_g, norm_final_g, conv_w_in, conv_b_in, conv_dw_w, conv_dw_b, conv_ln_g, conv_ln_b, conv_w_out, conv_b_out, attn_w_qkv, attn_w_o, peer_w_q, peer_sub_keys, peer_u, peer_v, ple_w_proj, ple_w_gate):
    n_b, seq, d = x_prompt.shape
    n_d, dec_seq, _ = x_sample.shape
    assert d == D_MODEL and dec_seq == 1 and seq % TOKEN_TILE == 0
    depth = norm_mix_g.shape[0]
    n_p = n_b * seq
    tp = -(-(n_p + n_d) // TOKEN_TILE) * TOKEN_TILE
    pad = tp - n_p - n_d

    def pack(prompt_rows, sample_rows):
        tail = jnp.zeros((pad, prompt_rows.shape[1]), prompt_rows.dtype)
        return jnp.concatenate([prompt_rows, sample_rows.astype(prompt_rows.dtype), tail], axis=0)

    def row(vec):
        return vec.reshape(1, -1)

    h = pack(x_prompt.reshape(n_p, d), x_sample.reshape(n_d, d))
    conv_p, conv_s, kp, vp, ks, vs = [], [], [], [], [], []
    zero_bias = jnp.zeros((1, d), F32)
    for i in range(depth):
        j = i // 2
        if i % 2 == 0:
            glu = glu_in(h, row(norm_mix_g[i]), conv_w_in[j].astype(BF16), row(conv_b_in[j]))
            dwk = jnp.concatenate([conv_dw_w[j][::-1], jnp.zeros((1, d), F32)], axis=0)
            y_p, tail = conv_prompt(glu, n_b, seq, dwk, row(conv_dw_b[j]),
                                    row(conv_ln_g[j]), row(conv_ln_b[j]))
            glu_s = glu[n_p:n_p + n_d]
            y_s = conv_sample(glu_s, state_conv[j], conv_dw_w[j], row(conv_dw_b[j]),
                              row(conv_ln_g[j]), row(conv_ln_b[j]))
            conv_p.append(tail[:, HALO - (CONV_WIDTH - 1):, :])
            conv_s.append(jnp.concatenate([state_conv[j][:, 1:, :], glu_s[:, None, :]], axis=1))
            h = proj_residual(h, pack(y_p, y_s), conv_w_out[j].astype(BF16), row(conv_b_out[j]))
        else:
            wq_hi, wq_lo = _split(attn_w_qkv[j][:, :d])
            q, k, v = qkv_proj(h, row(norm_mix_g[i]), wq_hi, wq_lo, attn_w_qkv[j][:, d:].astype(BF16))
            o_p = moba_prompt(q, k, v, rel_bias, n_b, seq)
            o_s = moba_sample(q[n_p:n_p + n_d], k[n_p:n_p + n_d], v[n_p:n_p + n_d],
                              cache_k[j], cache_v[j], page_table, rel_bias)
            kp.append(k[:n_p].reshape(n_b, seq, N_HEADS, HEAD_DIM))
            vp.append(v[:n_p].reshape(n_b, seq, N_HEADS, HEAD_DIM))
            ks.append(k[n_p:n_p + n_d].reshape(n_d, 1, N_HEADS, HEAD_DIM))
            vs.append(v[n_p:n_p + n_d].reshape(n_d, 1, N_HEADS, HEAD_DIM))
            h = proj_residual(h, pack(o_p, o_s), attn_w_o[j].astype(BF16), zero_bias)
        h = peer_layer(h, row(norm_ffn_g[i]), peer_w_q[i], peer_sub_keys[i], peer_u[i], peer_v[i])
        p_all = pack(p_prompt[i].reshape(n_p, -1), p_sample[i].reshape(n_d, -1))
        h = ple(h, p_all, row(norm_ple_g[i]), ple_w_gate[i].astype(BF16),
                ple_w_proj[i].astype(BF16), row(norm_final_g), final_norm=(i == depth - 1))
    y_prompt = h[:n_p].reshape(n_b, seq, d)
    y_sample = h[n_p:n_p + n_d].reshape(n_d, 1, d)
    return (y_prompt, y_sample, jnp.stack(conv_p), jnp.stack(conv_s),
            jnp.stack(kp), jnp.stack(vp), jnp.stack(ks), jnp.stack(vs))
```

```python
import functools
import math

import numpy as np
import jax
import jax.numpy as jnp
from jax import lax
from jax.experimental import pallas as pl
from jax.experimental.pallas import tpu as pltpu

F32 = jnp.float32
BF16 = jnp.bfloat16
EPS = 1e-6
NEG_INF = float("-inf")

D_MODEL = 1024
N_HEADS = 16
HEAD_DIM = 64
CONV_WIDTH = 31
MOBA_BLOCK = 256
MOBA_TOPK = 3
T5_BUCKETS = 32
T5_MAX_DIST = 128
PEER_HEADS = 8
PEER_N_KEYS = 128
PEER_KEY_HALF = 64
PEER_TOPK = 16
PAGE_SIZE = 128

TOKEN_TILE = 512
ROUTE_TILE = 256
ROUTE_HEADS = 2
EXPERT_TILE = 2048
EXPERT_CHUNK = 2048
CONV_CHUNK = 64
HALO = 32
STAT_LANES = 128
SAMPLE_STEP_PAGES = 8
LANES = 128
BF16_ROWS = 16
VMEM_LIMIT = 56 * 1024 * 1024

CAND_COUNT = [PEER_TOPK // (j + 1) for j in range(PEER_TOPK)]
CAND_ROWS = [16] + [8] * (PEER_TOPK - 1)
CAND_TOTAL = sum(CAND_ROWS)


def _params(*sem):
    return pltpu.CompilerParams(dimension_semantics=sem, vmem_limit_bytes=VMEM_LIMIT)


def _dot(a, b):
    return jnp.dot(a, b, preferred_element_type=F32)


def _dot_nt(a, b):
    return lax.dot_general(a, b, (((1,), (1,)), ((), ())), preferred_element_type=F32)


def _split(x):
    hi = x.astype(BF16)
    lo = (x - hi.astype(F32)).astype(BF16)
    return hi, lo


def _dot3(a_hi, a_lo, b_hi, b_lo):
    return _dot(a_hi, b_hi) + _dot(a_hi, b_lo) + _dot(a_lo, b_hi)


def _rms(x, g):
    return x * lax.rsqrt(jnp.mean(x * x, axis=-1, keepdims=True) + EPS) * g


def _sigmoid(x):
    return 1.0 / (1.0 + jnp.exp(-x))


def _row_spec(tile, width):
    return pl.BlockSpec((tile, width), lambda i: (i, 0))


def _full_spec(shape):
    nd = len(shape)
    return pl.BlockSpec(shape, lambda *_: (0,) * nd)


def _glu_in_kernel(h_ref, g_ref, w_ref, b_ref, o_ref):
    a = _rms(h_ref[...], g_ref[...]).astype(BF16)
    z = _dot(a, w_ref[...]) + b_ref[...]
    o_ref[...] = z[:, :D_MODEL] * _sigmoid(z[:, D_MODEL:])


def glu_in(h, g, w_bf, b):
    tp = h.shape[0]
    return pl.pallas_call(
        _glu_in_kernel,
        grid=(tp // TOKEN_TILE,),
        in_specs=[_row_spec(TOKEN_TILE, D_MODEL), _full_spec((1, D_MODEL)),
                  _full_spec((D_MODEL, 2 * D_MODEL)), _full_spec((1, 2 * D_MODEL))],
        out_specs=_row_spec(TOKEN_TILE, D_MODEL),
        out_shape=jax.ShapeDtypeStruct((tp, D_MODEL), F32),
        compiler_params=_params("parallel"),
        name="glu_in",
    )(h, g, w_bf, b)


def _proj_residual_kernel(h_ref, y_ref, w_ref, b_ref, o_ref):
    o_ref[...] = h_ref[...] + _dot(y_ref[...].astype(BF16), w_ref[...]) + b_ref[...]


def proj_residual(h, y, w_bf, b):
    tp = h.shape[0]
    return pl.pallas_call(
        _proj_residual_kernel,
        grid=(tp // TOKEN_TILE,),
        in_specs=[_row_spec(TOKEN_TILE, D_MODEL), _row_spec(TOKEN_TILE, D_MODEL),
                  _full_spec((D_MODEL, D_MODEL)), _full_spec((1, D_MODEL))],
        out_specs=_row_spec(TOKEN_TILE, D_MODEL),
        out_shape=jax.ShapeDtypeStruct((tp, D_MODEL), F32),
        compiler_params=_params("parallel"),
        name="proj_residual",
    )(h, y, w_bf, b)


def _ple_kernel(h_ref, p_ref, g_ref, wg_ref, wp_ref, gf_ref, o_ref, *, final_norm):
    h = h_ref[...]
    gate = _sigmoid(_dot(_rms(h, g_ref[...]).astype(BF16), wg_ref[...]))
    out = h + gate * _dot(p_ref[...].astype(BF16), wp_ref[...])
    if final_norm:
        out = _rms(out, gf_ref[...])
    o_ref[...] = out


def ple(h, p, g, wg_bf, wp_bf, g_final, final_norm):
    tp = h.shape[0]
    pdim = p.shape[1]
    return pl.pallas_call(
        functools.partial(_ple_kernel, final_norm=final_norm),
        grid=(tp // TOKEN_TILE,),
        in_specs=[_row_spec(TOKEN_TILE, D_MODEL), _row_spec(TOKEN_TILE, pdim),
                  _full_spec((1, D_MODEL)), _full_spec((D_MODEL, D_MODEL)),
                  _full_spec((pdim, D_MODEL)), _full_spec((1, D_MODEL))],
        out_specs=_row_spec(TOKEN_TILE, D_MODEL),
        out_shape=jax.ShapeDtypeStruct((tp, D_MODEL), F32),
        compiler_params=_params("parallel"),
        name="ple_final" if final_norm else "ple",
    )(h, p, g, wg_bf, wp_bf, g_final)


def _qkv_kernel(h_ref, g_ref, wq_hi_ref, wq_lo_ref, wkv_ref, q_ref, k_ref, v_ref):
    a_hi, a_lo = _split(_rms(h_ref[...], g_ref[...]))
    q_ref[...] = _dot3(a_hi, a_lo, wq_hi_ref[...], wq_lo_ref[...])
    kv = _dot(a_hi, wkv_ref[...])
    k_ref[...] = kv[:, :D_MODEL]
    v_ref[...] = kv[:, D_MODEL:]


def qkv_proj(h, g, wq_hi, wq_lo, wkv_bf):
    tp = h.shape[0]
    out = jax.ShapeDtypeStruct((tp, D_MODEL), F32)
    return pl.pallas_call(
        _qkv_kernel,
        grid=(tp // TOKEN_TILE,),
        in_specs=[_row_spec(TOKEN_TILE, D_MODEL), _full_spec((1, D_MODEL)),
                  _full_spec((D_MODEL, D_MODEL)), _full_spec((D_MODEL, D_MODEL)),
                  _full_spec((D_MODEL, 2 * D_MODEL))],
        out_specs=[_row_spec(TOKEN_TILE, D_MODEL)] * 3,
        out_shape=[out, out, out],
        compiler_params=_params("parallel"),
        name="qkv_proj",
    )(h, g, wq_hi, wq_lo, wkv_bf)


def _ln_silu(y, g, b):
    mu = jnp.mean(y, axis=-1, keepdims=True)
    yc = y - mu
    y = yc * lax.rsqrt(jnp.mean(yc * yc, axis=-1, keepdims=True) + EPS) * g + b
    return y * _sigmoid(y)


def _conv_prompt_kernel(x_ref, dwk_ref, dwb_ref, lng_ref, lnb_ref, y_ref, st_ref, buf_ref):
    tile = x_ref.shape[0]

    @pl.when(pl.program_id(1) == 0)
    def _():
        buf_ref[0:HALO, :] = jnp.zeros((HALO, D_MODEL), F32)

    buf_ref[HALO:HALO + tile, :] = x_ref[...]

    def chunk(c, carry):
        t0 = pl.multiple_of(c * CONV_CHUNK, CONV_CHUNK)
        win = buf_ref[pl.ds(t0, CONV_CHUNK + HALO), :]
        acc = None
        for s in range(8):
            part = None
            for u in range(4):
                k = 8 * u + s
                term = win[24 - 8 * u:24 - 8 * u + CONV_CHUNK + 8, :] * dwk_ref[k:k + 1, :]
                part = term if part is None else part + term
            if s:
                part = pltpu.roll(part, s, 0)
            part = part[8:8 + CONV_CHUNK, :]
            acc = part if acc is None else acc + part
        y = _ln_silu(acc + dwb_ref[...], lng_ref[...], lnb_ref[...])
        y_ref[pl.ds(t0, CONV_CHUNK), :] = y.astype(y_ref.dtype)
        return carry

    lax.fori_loop(0, tile // CONV_CHUNK, chunk, 0)
    tail = buf_ref[tile:tile + HALO, :]
    st_ref[0] = tail
    buf_ref[0:HALO, :] = tail


def conv_prompt(glu, n_b, seq, dwk, dwb, lng, lnb):
    nt = seq // TOKEN_TILE
    return pl.pallas_call(
        _conv_prompt_kernel,
        grid=(n_b, nt),
        in_specs=[pl.BlockSpec((TOKEN_TILE, D_MODEL), lambda n, t: (n * nt + t, 0)),
                  _full_spec((32, D_MODEL)), _full_spec((1, D_MODEL)),
                  _full_spec((1, D_MODEL)), _full_spec((1, D_MODEL))],
        out_specs=[pl.BlockSpec((TOKEN_TILE, D_MODEL), lambda n, t: (n * nt + t, 0)),
                   pl.BlockSpec((1, HALO, D_MODEL), lambda n, t: (n, 0, 0))],
        out_shape=[jax.ShapeDtypeStruct((n_b * seq, D_MODEL), BF16),
                   jax.ShapeDtypeStruct((n_b, HALO, D_MODEL), F32)],
        scratch_shapes=[pltpu.VMEM((TOKEN_TILE + HALO, D_MODEL), F32)],
        compiler_params=_params("arbitrary", "arbitrary"),
        name="conv_prompt",
    )(glu, dwk, dwb, lng, lnb)


def _conv_sample_kernel(x_ref, st_ref, dw_ref, dwb_ref, lng_ref, lnb_ref, y_ref):
    hist = jnp.sum(st_ref[...] * dw_ref[0:CONV_WIDTH - 1, :][None], axis=1)
    y = hist + x_ref[...] * dw_ref[CONV_WIDTH - 1:CONV_WIDTH, :] + dwb_ref[...]
    y_ref[...] = _ln_silu(y, lng_ref[...], lnb_ref[...]).astype(y_ref.dtype)


def conv_sample(glu_s, state, dw, dwb, lng, lnb):
    n_d = glu_s.shape[0]
    blk = 32 if n_d % 32 == 0 else n_d
    return pl.pallas_call(
        _conv_sample_kernel,
        grid=(n_d // blk,),
        in_specs=[_row_spec(blk, D_MODEL),
                  pl.BlockSpec((blk, CONV_WIDTH - 1, D_MODEL), lambda i: (i, 0, 0)),
                  _full_spec((CONV_WIDTH, D_MODEL)), _full_spec((1, D_MODEL)),
                  _full_spec((1, D_MODEL)), _full_spec((1, D_MODEL))],
        out_specs=_row_spec(blk, D_MODEL),
        out_shape=jax.ShapeDtypeStruct((n_d, D_MODEL), BF16),
        compiler_params=_params("parallel"),
        name="conv_sample",
    )(glu_s, state, dw, dwb, lng, lnb)


def _extract_top(w, tie_key, vals_ref, first_ref):
    def body(r, w):
        m = jnp.max(w, axis=1, keepdims=True)
        first = jnp.min(jnp.where(w == m, tie_key, 1e9), axis=1, keepdims=True)
        for g in range(w.shape[0]):
            if vals_ref is not None:
                vals_ref[g, pl.ds(r, 1), :] = m[g]
            if first_ref is not None:
                first_ref[g, pl.ds(r, 1), :] = first[g]
        return jnp.where(tie_key == first, NEG_INF, w)

    return lax.fori_loop(0, PEER_TOPK, body, w)


def _peer_route_kernel(x_ref, g_ref, wq_hi_ref, wq_lo_ref, keys_ref, cpos_ref, cneg_ref,
                       xt_ref, rank2_ref, e2_ref, nsel_ref, c1_ref,
                       q_scr, s_scr, v_scr, i_scr, c_scr):
    tile = x_ref.shape[0]
    xn_t = _rms(x_ref[...], g_ref[...]).T
    x_hi, x_lo = _split(xn_t)
    xt_ref[...] = x_hi
    q_scr[...] = (_dot(wq_hi_ref[...], x_hi) + _dot(wq_hi_ref[...], x_lo)
                  + _dot(wq_lo_ref[...], x_hi))
    key_iota = lax.broadcasted_iota(jnp.int32, (PEER_N_KEYS, tile), 0).astype(F32)
    cpos = cpos_ref[...]

    def head_group(hg, carry):
        for g in range(2 * ROUTE_HEADS):
            i = 2 * ROUTE_HEADS * hg + g
            qs = q_scr[pl.ds(pl.multiple_of(i * PEER_KEY_HALF, PEER_KEY_HALF), PEER_KEY_HALF), :]
            s_scr[g] = jnp.dot(keys_ref[i], qs, precision=lax.Precision.HIGHEST,
                               preferred_element_type=F32)
        _extract_top(s_scr[...], key_iota, v_scr, i_scr)
        for hh in range(ROUTE_HEADS):
            v2_16 = v_scr[2 * hh + 1]
            v2_8 = v_scr[2 * hh + 1, 0:8, :]
            c_scr[hh] = jnp.concatenate(
                [v_scr[2 * hh, j:j + 1, :] + (v2_16 if j == 0 else v2_8) for j in range(PEER_TOPK)],
                axis=0) + cneg_ref[...]
        left = _extract_top(c_scr[...], cpos, None, None)
        for hh in range(ROUTE_HEADS):
            h = ROUTE_HEADS * hg + hh
            max1 = v_scr[2 * hh, 0:1, :]
            max2 = v_scr[2 * hh + 1, 0:1, :]
            cand = c_scr[hh]
            picked = jnp.where((left[hh] == NEG_INF) & (cand > NEG_INF), 1.0, 0.0)
            z = jnp.sum(picked * jnp.exp(cand - (max1 + max2)), axis=0, keepdims=True)
            nsel = jnp.zeros((PEER_N_KEYS, tile), F32)
            rank2 = jnp.full((PEER_N_KEYS, tile), float(PEER_TOPK), F32)
            row = 0
            for j in range(PEER_TOPK):
                n_j = jnp.sum(picked[row:row + CAND_ROWS[j], :], axis=0, keepdims=True)
                nsel = jnp.where(key_iota == i_scr[2 * hh, j:j + 1, :], n_j, nsel)
                rank2 = jnp.where(key_iota == i_scr[2 * hh + 1, j:j + 1, :], float(j), rank2)
                row += CAND_ROWS[j]
            rank2_ref[h] = rank2.astype(rank2_ref.dtype)
            nsel_ref[h] = nsel
            e2_ref[h] = jnp.exp(s_scr[2 * hh + 1] - max2).astype(e2_ref.dtype)
            c1_ref[h] = jnp.exp(s_scr[2 * hh] - max1) / z
        return carry

    lax.fori_loop(0, PEER_HEADS // ROUTE_HEADS, head_group, 0)


def _cand_constants(tile):
    pos, neg = [], []
    for j in range(PEER_TOPK):
        for l in range(CAND_ROWS[j]):
            pos.append(16.0 * j + l)
            neg.append(0.0 if l < CAND_COUNT[j] else NEG_INF)
    pos = np.broadcast_to(np.asarray(pos, np.float32)[:, None], (CAND_TOTAL, tile))
    neg = np.broadcast_to(np.asarray(neg, np.float32)[:, None], (CAND_TOTAL, tile))
    return jnp.asarray(pos), jnp.asarray(neg)


def peer_route(h, g, wq_t_hi, wq_t_lo, keys):
    tp = h.shape[0]
    cpos, cneg = _cand_constants(ROUTE_TILE)
    dense = jax.ShapeDtypeStruct((PEER_HEADS, PEER_N_KEYS, tp), F32)
    dense_bf = jax.ShapeDtypeStruct((PEER_HEADS, PEER_N_KEYS, tp), BF16)
    dense_spec = pl.BlockSpec((PEER_HEADS, PEER_N_KEYS, ROUTE_TILE), lambda i: (0, 0, i))
    return pl.pallas_call(
        _peer_route_kernel,
        grid=(tp // ROUTE_TILE,),
        in_specs=[_row_spec(ROUTE_TILE, D_MODEL), _full_spec((1, D_MODEL)),
                  _full_spec((D_MODEL, D_MODEL)), _full_spec((D_MODEL, D_MODEL)),
                  _full_spec((2 * PEER_HEADS, PEER_N_KEYS, PEER_KEY_HALF)),
                  _full_spec((CAND_TOTAL, ROUTE_TILE)), _full_spec((CAND_TOTAL, ROUTE_TILE))],
        out_specs=[pl.BlockSpec((D_MODEL, ROUTE_TILE), lambda i: (0, i)),
                   dense_spec, dense_spec, dense_spec, dense_spec],
        out_shape=[jax.ShapeDtypeStruct((D_MODEL, tp), BF16), dense_bf, dense_bf, dense, dense],
        scratch_shapes=[pltpu.VMEM((D_MODEL, ROUTE_TILE), F32),
                        pltpu.VMEM((2 * ROUTE_HEADS, PEER_N_KEYS, ROUTE_TILE), F32),
                        pltpu.VMEM((2 * ROUTE_HEADS, PEER_TOPK, ROUTE_TILE), F32),
                        pltpu.VMEM((2 * ROUTE_HEADS, PEER_TOPK, ROUTE_TILE), F32),
                        pltpu.VMEM((ROUTE_HEADS, CAND_TOTAL, ROUTE_TILE), F32)],
        compiler_params=_params("parallel"),
        name="peer_route",
    )(h, g, wq_t_hi, wq_t_lo, keys, cpos, cneg)


def _peer_dense_kernel(xt_ref, u_ref, v_ref, rank2_in, e2_in, nsel_ref, c1_ref, h_ref,
                       o_ref, act_scr, g_scr, w_scr, acc_scr, rank2_ref, e2_ref):
    j = pl.program_id(1)

    @pl.when(j == 0)
    def _():
        acc_scr[...] = jnp.zeros(acc_scr.shape, F32)
        rank2_ref[...] = rank2_in[...]
        e2_ref[...] = e2_in[...]

    pack = BF16_ROWS
    zero = jnp.zeros((1, LANES), BF16)
    n_cols = xt_ref.shape[1] // LANES
    a_per_chunk = EXPERT_CHUNK // PEER_N_KEYS
    for chunk in range(EXPERT_TILE // EXPERT_CHUNK):
        experts = slice(chunk * EXPERT_CHUNK, (chunk + 1) * EXPERT_CHUNK)
        for a_local in range(chunk * a_per_chunk, (chunk + 1) * a_per_chunk):
            a_row = slice(a_local, a_local + 1)
            rows = slice(a_local * PEER_N_KEYS, (a_local + 1) * PEER_N_KEYS)
            for col in range(n_cols):
                cols = slice(col * LANES, (col + 1) * LANES)
                gate = None
                for h in range(PEER_HEADS):
                    n_row = jnp.broadcast_to(nsel_ref[h, a_row, cols], (pack, LANES)).astype(BF16)
                    c_row = jnp.broadcast_to(c1_ref[h, a_row, cols], (pack, LANES)).astype(BF16)
                    n_all = jnp.concatenate([n_row] * (PEER_N_KEYS // pack), axis=0)
                    c_all = jnp.concatenate([c_row] * (PEER_N_KEYS // pack), axis=0)
                    term = jnp.where(rank2_ref[h, :, cols] < n_all, e2_ref[h, :, cols] * c_all, zero)
                    gate = term if gate is None else gate + term
                g_scr[rows, cols] = gate
        act_scr[experts, :] = _dot(u_ref[experts, :], xt_ref[...])
        for a_local in range(chunk * a_per_chunk, (chunk + 1) * a_per_chunk):
            rows = slice(a_local * PEER_N_KEYS, (a_local + 1) * PEER_N_KEYS)
            for col in range(n_cols):
                cols = slice(col * LANES, (col + 1) * LANES)
                act = act_scr[rows, cols]
                gelu = 0.5 * act * (1.0 + lax.erf(act * (1.0 / math.sqrt(2.0))))
                w_scr[rows, cols] = gelu.astype(BF16) * g_scr[rows, cols]
        acc_scr[...] += _dot(v_ref[:, experts], w_scr[experts, :])

    @pl.when(j == pl.num_programs(1) - 1)
    def _():
        o_ref[...] = h_ref[...] + acc_scr[...].T


def peer_dense(h, xt, u_bf, v_bf, rank2, e2, nsel, c1):
    tp = h.shape[0]
    n_exp = u_bf.shape[0]
    dense_spec = pl.BlockSpec((PEER_HEADS, PEER_N_KEYS, TOKEN_TILE), lambda i, j: (0, 0, i))
    a_spec = pl.BlockSpec((PEER_HEADS, EXPERT_TILE // PEER_N_KEYS, TOKEN_TILE),
                          lambda i, j: (0, j, i))
    return pl.pallas_call(
        _peer_dense_kernel,
        grid=(tp // TOKEN_TILE, n_exp // EXPERT_TILE),
        in_specs=[pl.BlockSpec((D_MODEL, TOKEN_TILE), lambda i, j: (0, i)),
                  pl.BlockSpec((EXPERT_TILE, D_MODEL), lambda i, j: (j, 0)),
                  pl.BlockSpec((D_MODEL, EXPERT_TILE), lambda i, j: (0, j)),
                  dense_spec, dense_spec, a_spec, a_spec,
                  pl.BlockSpec((TOKEN_TILE, D_MODEL), lambda i, j: (i, 0))],
        out_specs=pl.BlockSpec((TOKEN_TILE, D_MODEL), lambda i, j: (i, 0)),
        out_shape=jax.ShapeDtypeStruct((tp, D_MODEL), F32),
        scratch_shapes=[pltpu.VMEM((EXPERT_TILE, TOKEN_TILE), F32),
                        pltpu.VMEM((EXPERT_TILE, TOKEN_TILE), BF16),
                        pltpu.VMEM((EXPERT_TILE, TOKEN_TILE), BF16),
                        pltpu.VMEM((D_MODEL, TOKEN_TILE), F32),
                        pltpu.VMEM((PEER_HEADS, PEER_N_KEYS, TOKEN_TILE), BF16),
                        pltpu.VMEM((PEER_HEADS, PEER_N_KEYS, TOKEN_TILE), BF16)],
        compiler_params=_params("parallel", "arbitrary"),
        name="peer_dense",
    )(xt, u_bf, v_bf, rank2, e2, nsel, c1, h)


def peer_layer(h, g, wq, sub_keys, u, v):
    wq_t_hi, wq_t_lo = _split(wq.T)
    keys = sub_keys.reshape(2 * PEER_HEADS, PEER_N_KEYS, PEER_KEY_HALF)
    xt, rank2, e2, nsel, c1 = peer_route(h, g, wq_t_hi, wq_t_lo, keys)
    return peer_dense(h, xt, u.astype(BF16), v.astype(BF16).T, rank2, e2, nsel, c1)


def _bucket_starts():
    max_exact = T5_BUCKETS // 2
    d = np.arange(0, 4 * T5_MAX_DIST)
    df = np.maximum(d, 1).astype(np.float64)
    large = max_exact + (np.log(df / max_exact) / math.log(T5_MAX_DIST / max_exact)
                         * (T5_BUCKETS - max_exact)).astype(np.int64)
    bucket = np.where(d < max_exact, d, np.minimum(large, T5_BUCKETS - 1))
    starts = []
    for b in range(T5_BUCKETS):
        idx = np.nonzero(bucket == b)[0]
        if idx.size:
            starts.append((int(idx[0]), b))
    return starts


BUCKET_STARTS = _bucket_starts()


def _bias_from_dist(dist, table):
    out = None
    for start, b in BUCKET_STARTS:
        val = table(b)
        out = val + jnp.zeros(dist.shape, F32) if out is None else jnp.where(dist >= start, val, out)
    return out


def _moba_prompt_kernel(rb_ref, q_ref, k_ref, v_ref, o_ref,
                        kb_scr, vt_scr, km_scr, bias_scr, sel_scr, m_scr, l_scr, acc_scr):
    pair = pl.program_id(1)
    qt = pl.program_id(2)
    n_blk = k_ref.shape[0] // MOBA_BLOCK
    blk = MOBA_BLOCK
    lane = lax.broadcasted_iota(jnp.int32, (1, 2 * HEAD_DIM), 1)

    @pl.when(qt == 0)
    def _():
        for b in range(n_blk):
            kt = k_ref[b * blk:(b + 1) * blk, :]
            kb_scr[b] = kt.astype(BF16)
            km_scr[b:b + 1, :] = jnp.mean(kt, axis=0, keepdims=True)
            vt_scr[b] = v_ref[b * blk:(b + 1) * blk, :].T.astype(BF16)
        ik = lax.broadcasted_iota(jnp.int32, (blk, blk), 0)
        iq = lax.broadcasted_iota(jnp.int32, (blk, blk), 1)
        for hh in range(2):
            head = 2 * pair + hh
            for delta in range(2):
                dist = iq - ik + delta * blk
                bias = _bias_from_dist(jnp.maximum(dist, 0), lambda b: rb_ref[b, head])
                bias_scr[hh, delta] = jnp.where(dist >= 0, bias, NEG_INF)

    own = qt
    blk_iota = lax.broadcasted_iota(jnp.int32, (n_blk, blk), 0)
    heads = range(2)
    qh_bf, far_bias = [], []
    for hh in heads:
        head = 2 * pair + hh
        in_head = (lane >= hh * HEAD_DIM) & (lane < (hh + 1) * HEAD_DIM)
        qh = jnp.where(in_head, q_ref[...] * (HEAD_DIM ** -0.5), 0.0)
        qh_bf.append(qh.astype(BF16))
        km = jnp.where(in_head, km_scr[...], 0.0)
        gate = lax.dot_general(km, qh, (((1,), (1,)), ((), ())),
                               precision=lax.Precision.HIGHEST,
                               preferred_element_type=F32)
        gate = jnp.where(blk_iota < own, gate, NEG_INF)
        rank = jnp.zeros((n_blk, blk), F32)
        for b in range(n_blk):
            gb = gate[b:b + 1, :]
            ahead = (gb > gate) | ((gb == gate) & (b < blk_iota))
            rank = rank + jnp.where(ahead, 1.0, 0.0)
        sel_scr[hh] = jnp.where((blk_iota < own) & (rank < float(MOBA_TOPK)), 1.0, 0.0)
        far_bias.append(rb_ref[T5_BUCKETS - 1, head])

    def step(hh, blocks, first):
        v_rows = slice(hh * HEAD_DIM, (hh + 1) * HEAD_DIM)
        scores = []
        for kb, bias in blocks:
            s = _dot_nt(kb_scr[kb], qh_bf[hh])
            if first:
                s = s + bias
            else:
                s = s + (far_bias[hh] if bias is None else bias)
                s = jnp.where(sel_scr[hh, pl.ds(kb, 1), :] > 0.0, s, NEG_INF)
            scores.append(s)
        m_new = None if first else m_scr[hh]
        for s in scores:
            m_blk = jnp.max(s, axis=0, keepdims=True)
            m_new = m_blk if m_new is None else jnp.maximum(m_new, m_blk)
        pv, p_sum = None, None
        for (kb, _), s in zip(blocks, scores):
            p = jnp.exp(s - m_new)
            part = _dot(vt_scr[kb, v_rows, :], p.astype(BF16))
            part_sum = jnp.sum(p, axis=0, keepdims=True)
            pv = part if pv is None else pv + part
            p_sum = part_sum if p_sum is None else p_sum + part_sum
        if first:
            l_scr[hh] = p_sum
            acc_scr[v_rows, :] = pv
        else:
            alpha = jnp.exp(m_scr[hh] - m_new)
            l_scr[hh] = alpha * l_scr[hh] + p_sum
            acc_scr[v_rows, :] = alpha * acc_scr[v_rows, :] + pv
        m_scr[hh] = m_new

    for hh in heads:
        step(hh, [(own, bias_scr[hh, 0])], True)

    @pl.when(own >= 1)
    def _():
        for hh in heads:
            step(hh, [(own - 1, bias_scr[hh, 1])], False)

    n_far = jnp.maximum(own - 1, 0)

    def far_pair(i, carry):
        for hh in heads:
            step(hh, [(2 * i, None), (2 * i + 1, None)], False)
        return carry

    lax.fori_loop(0, n_far // 2, far_pair, 0)

    @pl.when(n_far % 2 == 1)
    def _():
        for hh in heads:
            step(hh, [(n_far - 1, None)], False)
    for hh in heads:
        v_rows = slice(hh * HEAD_DIM, (hh + 1) * HEAD_DIM)
        acc_scr[v_rows, :] = acc_scr[v_rows, :] / l_scr[hh]
    o_ref[...] = acc_scr[...].T


def moba_prompt(q, k, v, rel_bias, n_b, seq):
    assert seq % MOBA_BLOCK == 0
    nq = seq // MOBA_BLOCK
    width = 2 * HEAD_DIM
    grid_spec = pltpu.PrefetchScalarGridSpec(
        num_scalar_prefetch=0,
        grid=(n_b, N_HEADS // 2, nq),
        in_specs=[pl.BlockSpec(memory_space=pltpu.SMEM),
                  pl.BlockSpec((MOBA_BLOCK, width), lambda n, p, t: (n * nq + t, p)),
                  pl.BlockSpec((seq, width), lambda n, p, t: (n, p)),
                  pl.BlockSpec((seq, width), lambda n, p, t: (n, p))],
        out_specs=pl.BlockSpec((MOBA_BLOCK, width), lambda n, p, t: (n * nq + t, p)),
        scratch_shapes=[pltpu.VMEM((nq, MOBA_BLOCK, width), BF16),
                        pltpu.VMEM((nq, width, MOBA_BLOCK), BF16),
                        pltpu.VMEM((nq, width), F32),
                        pltpu.VMEM((2, 2, MOBA_BLOCK, MOBA_BLOCK), F32),
                        pltpu.VMEM((2, nq, MOBA_BLOCK), F32),
                        pltpu.VMEM((2, 1, MOBA_BLOCK), F32),
                        pltpu.VMEM((2, 1, MOBA_BLOCK), F32),
                        pltpu.VMEM((width, MOBA_BLOCK), F32)],
    )
    return pl.pallas_call(
        _moba_prompt_kernel,
        grid_spec=grid_spec,
        out_shape=jax.ShapeDtypeStruct((n_b * seq, D_MODEL), F32),
        compiler_params=_params("parallel", "parallel", "arbitrary"),
        name="moba_prompt",
    )(rel_bias, q, k, v)


def _dense_page(page_ref):
    pairs = []
    for hp in range(N_HEADS // 2):
        even = page_ref[0, pl.ds(2 * hp, PAGE_SIZE, stride=N_HEADS), :]
        odd = page_ref[0, pl.ds(2 * hp + 1, PAGE_SIZE, stride=N_HEADS), :]
        pairs.append(jnp.concatenate([even, odd], axis=1))
    return jnp.concatenate(pairs, axis=1)


def _moba_sample_kernel(pt_ref, rbt_ref, q_ref, kn_ref, vn_ref, *refs, pages_per_step):
    k_refs = refs[:pages_per_step]
    v_refs = refs[pages_per_step:2 * pages_per_step]
    o_ref, gate_scr, m_scr, l_scr, acc_scr = refs[2 * pages_per_step:]
    blk = MOBA_BLOCK
    pages_per_blk = blk // PAGE_SIZE
    blks_per_step = pages_per_step // pages_per_blk
    n_blk = m_scr.shape[0]
    step = pl.program_id(1)
    stat = (N_HEADS, STAT_LANES)
    head_of_lane = lax.broadcasted_iota(jnp.int32, (N_HEADS, D_MODEL), 1) // HEAD_DIM
    head_of_row = lax.broadcasted_iota(jnp.int32, (N_HEADS, D_MODEL), 0)
    diag = head_of_lane == head_of_row
    qbd = jnp.where(diag, q_ref[0] * (HEAD_DIM ** -0.5), 0.0)
    qbd_bf = qbd.astype(BF16)
    key_iota = lax.broadcasted_iota(jnp.int32, (N_HEADS, blk), 1)

    for bl in range(blks_per_step):
        b = step * blks_per_step + bl
        pages = range(pages_per_blk * bl, pages_per_blk * (bl + 1))
        k_blk = jnp.concatenate([_dense_page(k_refs[i]) for i in pages], axis=0)
        v_blk = jnp.concatenate([_dense_page(v_refs[i]) for i in pages], axis=0)
        gate = jnp.sum(qbd * jnp.mean(k_blk, axis=0, keepdims=True), axis=1, keepdims=True)
        s = _dot_nt(qbd_bf, k_blk.astype(BF16))
        dist = (n_blk - b) * blk - key_iota
        s = s + _bias_from_dist(dist, lambda t: rbt_ref[:, t:t + 1])
        m = jnp.max(s, axis=1, keepdims=True)
        p = jnp.exp(s - m)
        gate_scr[b] = jnp.broadcast_to(gate, stat)
        m_scr[b] = jnp.broadcast_to(m, stat)
        l_scr[b] = jnp.broadcast_to(jnp.sum(p, axis=1, keepdims=True), stat)
        acc_scr[b] = _dot(p.astype(BF16), v_blk.astype(BF16))

    @pl.when(step == pl.num_programs(1) - 1)
    def _():
        gates = [gate_scr[i] for i in range(n_blk)]
        s_self = jnp.sum(qbd * kn_ref[0], axis=1, keepdims=True) + rbt_ref[:, 0:1]
        m_tot = jnp.broadcast_to(s_self, stat)
        keep = []
        for i in range(n_blk):
            rank = jnp.zeros(stat, F32)
            for i2 in range(n_blk):
                if i2 == i:
                    continue
                ahead = (gates[i2] > gates[i]) if i2 > i else (gates[i2] >= gates[i])
                rank = rank + jnp.where(ahead, 1.0, 0.0)
            keep.append(rank < float(MOBA_TOPK))
            m_tot = jnp.where(keep[i], jnp.maximum(m_tot, m_scr[i]), m_tot)
        w_self = jnp.exp(s_self - m_tot)
        l_tot = w_self
        out = w_self[:, 0:1] * vn_ref[0]
        for i in range(n_blk):
            w_i = jnp.where(keep[i], jnp.exp(m_scr[i] - m_tot), 0.0)
            l_tot = l_tot + w_i * l_scr[i]
            out = out + w_i[:, 0:1] * acc_scr[i]
        out = jnp.where(diag, out / l_tot[:, 0:1], 0.0)
        o_ref[0] = jnp.sum(out, axis=0, keepdims=True)


def moba_sample(q, k_new, v_new, cache_k, cache_v, page_table, rel_bias):
    n_d, n_pages = page_table.shape
    pages_per_blk = MOBA_BLOCK // PAGE_SIZE
    n_blk = n_pages // pages_per_blk
    assert n_pages % pages_per_blk == 0
    pages_per_step = SAMPLE_STEP_PAGES if n_pages % SAMPLE_STEP_PAGES == 0 else pages_per_blk
    n_phys = cache_k.shape[0]
    cache_k = cache_k.reshape(n_phys, PAGE_SIZE * N_HEADS, HEAD_DIM)
    cache_v = cache_v.reshape(n_phys, PAGE_SIZE * N_HEADS, HEAD_DIM)
    row = pl.BlockSpec((1, 1, D_MODEL), lambda n, t, pt: (n, 0, 0))

    def page(i):
        return pl.BlockSpec((1, PAGE_SIZE * N_HEADS, HEAD_DIM),
                            lambda n, t, pt: (pt[n, t * pages_per_step + i], 0, 0))

    pages = [page(i) for i in range(pages_per_step)]
    grid_spec = pltpu.PrefetchScalarGridSpec(
        num_scalar_prefetch=1,
        grid=(n_d, n_pages // pages_per_step),
        in_specs=[pl.BlockSpec((N_HEADS, T5_BUCKETS), lambda n, t, pt: (0, 0)),
                  row, row, row] + pages + pages,
        out_specs=row,
        scratch_shapes=[pltpu.VMEM((n_blk, N_HEADS, STAT_LANES), F32),
                        pltpu.VMEM((n_blk, N_HEADS, STAT_LANES), F32),
                        pltpu.VMEM((n_blk, N_HEADS, STAT_LANES), F32),
                        pltpu.VMEM((n_blk, N_HEADS, D_MODEL), F32)],
    )
    out = pl.pallas_call(
        functools.partial(_moba_sample_kernel, pages_per_step=pages_per_step),
        grid_spec=grid_spec,
        out_shape=jax.ShapeDtypeStruct((n_d, 1, D_MODEL), F32),
        compiler_params=_params("parallel", "arbitrary"),
        name="moba_sample",
    )(page_table, rel_bias.T, q[:, None, :], k_new[:, None, :], v_new[:, None, :],
      *([cache_k] * pages_per_step), *([cache_v] * pages_per_step))
    return out[:, 0, :]


def kernel(x_prompt, x_sample, state_conv, cache_k, cache_v, page_table, p_prompt, p_sample, rel_bias, norm_mix_g, norm_ffn_g, norm_ple_g, norm_final_g, conv_w_in, conv_b_in, conv_dw_w, conv_dw_b, conv_ln_g, conv_ln_b, conv_w_out, conv_b_out, attn_w_qkv, attn_w_o, peer_w_q, peer_sub_keys, peer_u, peer_v, ple_w_proj, ple_w_gate):
    n_b, seq, d = x_prompt.shape
    n_d, dec_seq, _ = x_sample.shape
    assert d == D_MODEL and dec_seq == 1 and seq % TOKEN_TILE == 0
    depth = norm_mix_g.shape[0]
    n_p = n_b * seq
    tp = -(-(n_p + n_d) // TOKEN_TILE) * TOKEN_TILE
    pad = tp - n_p - n_d

    def pack(prompt_rows, sample_rows):
        tail = jnp.zeros((pad, prompt_rows.shape[1]), prompt_rows.dtype)
        return jnp.concatenate([prompt_rows, sample_rows.astype(prompt_rows.dtype), tail], axis=0)

    def row(vec):
        return vec.reshape(1, -1)

    h = pack(x_prompt.reshape(n_p, d), x_sample.reshape(n_d, d))
    conv_p, conv_s, kp, vp, ks, vs = [], [], [], [], [], []
    zero_bias = jnp.zeros((1, d), F32)
    for i in range(depth):
        j = i // 2
        if i % 2 == 0:
            glu = glu_in(h, row(norm_mix_g[i]), conv_w_in[j].astype(BF16), row(conv_b_in[j]))
            dwk = jnp.concatenate([conv_dw_w[j][::-1], jnp.zeros((1, d), F32)], axis=0)
            y_p, tail = conv_prompt(glu, n_b, seq, dwk, row(conv_dw_b[j]),
                                    row(conv_ln_g[j]), row(conv_ln_b[j]))
            glu_s = glu[n_p:n_p + n_d]
            y_s = conv_sample(glu_s, state_conv[j], conv_dw_w[j], row(conv_dw_b[j]),
                              row(conv_ln_g[j]), row(conv_ln_b[j]))
            conv_p.append(tail[:, HALO - (CONV_WIDTH - 1):, :])
            conv_s.append(jnp.concatenate([state_conv[j][:, 1:, :], glu_s[:, None, :]], axis=1))
            h = proj_residual(h, pack(y_p, y_s), conv_w_out[j].astype(BF16), row(conv_b_out[j]))
        else:
            wq_hi, wq_lo = _split(attn_w_qkv[j][:, :d])
            q, k, v = qkv_proj(h, row(norm_mix_g[i]), wq_hi, wq_lo, attn_w_qkv[j][:, d:].astype(BF16))
            o_p = moba_prompt(q, k, v, rel_bias, n_b, seq)
            o_s = moba_sample(q[n_p:n_p + n_d], k[n_p:n_p + n_d], v[n_p:n_p + n_d],
                              cache_k[j], cache_v[j], page_table, rel_bias)
            kp.append(k[:n_p].reshape(n_b, seq, N_HEADS, HEAD_DIM))
            vp.append(v[:n_p].reshape(n_b, seq, N_HEADS, HEAD_DIM))
            ks.append(k[n_p:n_p + n_d].reshape(n_d, 1, N_HEADS, HEAD_DIM))
            vs.append(v[n_p:n_p + n_d].reshape(n_d, 1, N_HEADS, HEAD_DIM))
            h = proj_residual(h, pack(o_p, o_s), attn_w_o[j].astype(BF16), zero_bias)
        h = peer_layer(h, row(norm_ffn_g[i]), peer_w_q[i], peer_sub_keys[i], peer_u[i], peer_v[i])
        p_all = pack(p_prompt[i].reshape(n_p, -1), p_sample[i].reshape(n_d, -1))
        h = ple(h, p_all, row(norm_ple_g[i]), ple_w_gate[i].astype(BF16),
                ple_w_proj[i].astype(BF16), row(norm_final_g), final_norm=(i == depth - 1))
    y_prompt = h[:n_p].reshape(n_b, seq, d)
    y_sample = h[n_p:n_p + n_d].reshape(n_d, 1, d)
    return (y_prompt, y_sample, jnp.stack(conv_p), jnp.stack(conv_s),
            jnp.stack(kp), jnp.stack(vp), jnp.stack(ks), jnp.stack(vs))
```

```python
import functools
import math

import numpy as np
import jax
import jax.numpy as jnp
from jax import lax
from jax.experimental import pallas as pl
from jax.experimental.pallas import tpu as pltpu

F32 = jnp.float32
BF16 = jnp.bfloat16
EPS = 1e-6
NEG_INF = float("-inf")

D_MODEL = 1024
N_HEADS = 16
HEAD_DIM = 64
CONV_WIDTH = 31
MOBA_BLOCK = 256
MOBA_TOPK = 3
T5_BUCKETS = 32
T5_MAX_DIST = 128
PEER_HEADS = 8
PEER_N_KEYS = 128
PEER_KEY_HALF = 64
PEER_TOPK = 16
PAGE_SIZE = 128

TOKEN_TILE = 512
ROUTE_TILE = 256
ROUTE_HEADS = 2
EXPERT_TILE = 2048
EXPERT_CHUNK = 2048
CONV_CHUNK = 64
HALO = 32
STAT_LANES = 128
LANES = 128
BF16_ROWS = 16
VMEM_LIMIT = 56 * 1024 * 1024

CAND_COUNT = [PEER_TOPK // (j + 1) for j in range(PEER_TOPK)]
CAND_ROWS = [16] + [8] * (PEER_TOPK - 1)
CAND_TOTAL = sum(CAND_ROWS)


def _params(*sem):
    return pltpu.CompilerParams(dimension_semantics=sem, vmem_limit_bytes=VMEM_LIMIT)


def _dot(a, b):
    return jnp.dot(a, b, preferred_element_type=F32)


def _dot_nt(a, b):
    return lax.dot_general(a, b, (((1,), (1,)), ((), ())), preferred_element_type=F32)


def _split(x):
    hi = x.astype(BF16)
    lo = (x - hi.astype(F32)).astype(BF16)
    return hi, lo


def _dot3(a_hi, a_lo, b_hi, b_lo):
    return _dot(a_hi, b_hi) + _dot(a_hi, b_lo) + _dot(a_lo, b_hi)


def _rms(x, g):
    return x * lax.rsqrt(jnp.mean(x * x, axis=-1, keepdims=True) + EPS) * g


def _sigmoid(x):
    return 1.0 / (1.0 + jnp.exp(-x))


def _row_spec(tile, width):
    return pl.BlockSpec((tile, width), lambda i: (i, 0))


def _full_spec(shape):
    nd = len(shape)
    return pl.BlockSpec(shape, lambda *_: (0,) * nd)


def _glu_in_kernel(h_ref, g_ref, w_ref, b_ref, o_ref):
    a = _rms(h_ref[...], g_ref[...]).astype(BF16)
    z = _dot(a, w_ref[...]) + b_ref[...]
    o_ref[...] = z[:, :D_MODEL] * _sigmoid(z[:, D_MODEL:])


def glu_in(h, g, w_bf, b):
    tp = h.shape[0]
    return pl.pallas_call(
        _glu_in_kernel,
        grid=(tp // TOKEN_TILE,),
        in_specs=[_row_spec(TOKEN_TILE, D_MODEL), _full_spec((1, D_MODEL)),
                  _full_spec((D_MODEL, 2 * D_MODEL)), _full_spec((1, 2 * D_MODEL))],
        out_specs=_row_spec(TOKEN_TILE, D_MODEL),
        out_shape=jax.ShapeDtypeStruct((tp, D_MODEL), F32),
        compiler_params=_params("parallel"),
        name="glu_in",
    )(h, g, w_bf, b)


def _proj_residual_kernel(h_ref, y_ref, w_ref, b_ref, o_ref):
    o_ref[...] = h_ref[...] + _dot(y_ref[...].astype(BF16), w_ref[...]) + b_ref[...]


def proj_residual(h, y, w_bf, b):
    tp = h.shape[0]
    return pl.pallas_call(
        _proj_residual_kernel,
        grid=(tp // TOKEN_TILE,),
        in_specs=[_row_spec(TOKEN_TILE, D_MODEL), _row_spec(TOKEN_TILE, D_MODEL),
                  _full_spec((D_MODEL, D_MODEL)), _full_spec((1, D_MODEL))],
        out_specs=_row_spec(TOKEN_TILE, D_MODEL),
        out_shape=jax.ShapeDtypeStruct((tp, D_MODEL), F32),
        compiler_params=_params("parallel"),
        name="proj_residual",
    )(h, y, w_bf, b)


def _ple_kernel(h_ref, p_ref, g_ref, wg_ref, wp_ref, gf_ref, o_ref, *, final_norm):
    h = h_ref[...]
    gate = _sigmoid(_dot(_rms(h, g_ref[...]).astype(BF16), wg_ref[...]))
    out = h + gate * _dot(p_ref[...].astype(BF16), wp_ref[...])
    if final_norm:
        out = _rms(out, gf_ref[...])
    o_ref[...] = out


def ple(h, p, g, wg_bf, wp_bf, g_final, final_norm):
    tp = h.shape[0]
    pdim = p.shape[1]
    return pl.pallas_call(
        functools.partial(_ple_kernel, final_norm=final_norm),
        grid=(tp // TOKEN_TILE,),
        in_specs=[_row_spec(TOKEN_TILE, D_MODEL), _row_spec(TOKEN_TILE, pdim),
                  _full_spec((1, D_MODEL)), _full_spec((D_MODEL, D_MODEL)),
                  _full_spec((pdim, D_MODEL)), _full_spec((1, D_MODEL))],
        out_specs=_row_spec(TOKEN_TILE, D_MODEL),
        out_shape=jax.ShapeDtypeStruct((tp, D_MODEL), F32),
        compiler_params=_params("parallel"),
        name="ple_final" if final_norm else "ple",
    )(h, p, g, wg_bf, wp_bf, g_final)


def _qkv_kernel(h_ref, g_ref, wq_hi_ref, wq_lo_ref, wkv_ref, q_ref, k_ref, v_ref):
    a_hi, a_lo = _split(_rms(h_ref[...], g_ref[...]))
    q_ref[...] = _dot3(a_hi, a_lo, wq_hi_ref[...], wq_lo_ref[...])
    kv = _dot(a_hi, wkv_ref[...])
    k_ref[...] = kv[:, :D_MODEL]
    v_ref[...] = kv[:, D_MODEL:]


def qkv_proj(h, g, wq_hi, wq_lo, wkv_bf):
    tp = h.shape[0]
    out = jax.ShapeDtypeStruct((tp, D_MODEL), F32)
    return pl.pallas_call(
        _qkv_kernel,
        grid=(tp // TOKEN_TILE,),
        in_specs=[_row_spec(TOKEN_TILE, D_MODEL), _full_spec((1, D_MODEL)),
                  _full_spec((D_MODEL, D_MODEL)), _full_spec((D_MODEL, D_MODEL)),
                  _full_spec((D_MODEL, 2 * D_MODEL))],
        out_specs=[_row_spec(TOKEN_TILE, D_MODEL)] * 3,
        out_shape=[out, out, out],
        compiler_params=_params("parallel"),
        name="qkv_proj",
    )(h, g, wq_hi, wq_lo, wkv_bf)


def _ln_silu(y, g, b):
    mu = jnp.mean(y, axis=-1, keepdims=True)
    yc = y - mu
    y = yc * lax.rsqrt(jnp.mean(yc * yc, axis=-1, keepdims=True) + EPS) * g + b
    return y * _sigmoid(y)


def _conv_prompt_kernel(x_ref, dwk_ref, dwb_ref, lng_ref, lnb_ref, y_ref, st_ref, buf_ref):
    tile = x_ref.shape[0]

    @pl.when(pl.program_id(1) == 0)
    def _():
        buf_ref[0:HALO, :] = jnp.zeros((HALO, D_MODEL), F32)

    buf_ref[HALO:HALO + tile, :] = x_ref[...]

    def chunk(c, carry):
        t0 = pl.multiple_of(c * CONV_CHUNK, CONV_CHUNK)
        win = buf_ref[pl.ds(t0, CONV_CHUNK + HALO), :]
        acc = None
        for s in range(8):
            part = None
            for u in range(4):
                k = 8 * u + s
                term = win[24 - 8 * u:24 - 8 * u + CONV_CHUNK + 8, :] * dwk_ref[k:k + 1, :]
                part = term if part is None else part + term
            if s:
                part = pltpu.roll(part, s, 0)
            part = part[8:8 + CONV_CHUNK, :]
            acc = part if acc is None else acc + part
        y = _ln_silu(acc + dwb_ref[...], lng_ref[...], lnb_ref[...])
        y_ref[pl.ds(t0, CONV_CHUNK), :] = y.astype(y_ref.dtype)
        return carry

    lax.fori_loop(0, tile // CONV_CHUNK, chunk, 0)
    tail = buf_ref[tile:tile + HALO, :]
    st_ref[0] = tail
    buf_ref[0:HALO, :] = tail


def conv_prompt(glu, n_b, seq, dwk, dwb, lng, lnb):
    nt = seq // TOKEN_TILE
    return pl.pallas_call(
        _conv_prompt_kernel,
        grid=(n_b, nt),
        in_specs=[pl.BlockSpec((TOKEN_TILE, D_MODEL), lambda n, t: (n * nt + t, 0)),
                  _full_spec((32, D_MODEL)), _full_spec((1, D_MODEL)),
                  _full_spec((1, D_MODEL)), _full_spec((1, D_MODEL))],
        out_specs=[pl.BlockSpec((TOKEN_TILE, D_MODEL), lambda n, t: (n * nt + t, 0)),
                   pl.BlockSpec((1, HALO, D_MODEL), lambda n, t: (n, 0, 0))],
        out_shape=[jax.ShapeDtypeStruct((n_b * seq, D_MODEL), BF16),
                   jax.ShapeDtypeStruct((n_b, HALO, D_MODEL), F32)],
        scratch_shapes=[pltpu.VMEM((TOKEN_TILE + HALO, D_MODEL), F32)],
        compiler_params=_params("arbitrary", "arbitrary"),
        name="conv_prompt",
    )(glu, dwk, dwb, lng, lnb)


def _conv_sample_kernel(x_ref, st_ref, dw_ref, dwb_ref, lng_ref, lnb_ref, y_ref):
    hist = jnp.sum(st_ref[...] * dw_ref[0:CONV_WIDTH - 1, :][None], axis=1)
    y = hist + x_ref[...] * dw_ref[CONV_WIDTH - 1:CONV_WIDTH, :] + dwb_ref[...]
    y_ref[...] = _ln_silu(y, lng_ref[...], lnb_ref[...]).astype(y_ref.dtype)


def conv_sample(glu_s, state, dw, dwb, lng, lnb):
    n_d = glu_s.shape[0]
    blk = 32 if n_d % 32 == 0 else n_d
    return pl.pallas_call(
        _conv_sample_kernel,
        grid=(n_d // blk,),
        in_specs=[_row_spec(blk, D_MODEL),
                  pl.BlockSpec((blk, CONV_WIDTH - 1, D_MODEL), lambda i: (i, 0, 0)),
                  _full_spec((CONV_WIDTH, D_MODEL)), _full_spec((1, D_MODEL)),
                  _full_spec((1, D_MODEL)), _full_spec((1, D_MODEL))],
        out_specs=_row_spec(blk, D_MODEL),
        out_shape=jax.ShapeDtypeStruct((n_d, D_MODEL), BF16),
        compiler_params=_params("parallel"),
        name="conv_sample",
    )(glu_s, state, dw, dwb, lng, lnb)


def _extract_top(w, tie_key, vals_ref, first_ref):
    def body(r, w):
        m = jnp.max(w, axis=1, keepdims=True)
        first = jnp.min(jnp.where(w == m, tie_key, 1e9), axis=1, keepdims=True)
        for g in range(w.shape[0]):
            if vals_ref is not None:
                vals_ref[g, pl.ds(r, 1), :] = m[g]
            if first_ref is not None:
                first_ref[g, pl.ds(r, 1), :] = first[g]
        return jnp.where(tie_key == first, NEG_INF, w)

    return lax.fori_loop(0, PEER_TOPK, body, w)


def _peer_route_kernel(x_ref, g_ref, wq_hi_ref, wq_lo_ref, keys_ref, cpos_ref, cneg_ref,
                       xt_ref, rank2_ref, e2_ref, nsel_ref, c1_ref,
                       q_scr, s_scr, v_scr, i_scr, c_scr):
    tile = x_ref.shape[0]
    xn_t = _rms(x_ref[...], g_ref[...]).T
    x_hi, x_lo = _split(xn_t)
    xt_ref[...] = x_hi
    q_scr[...] = (_dot(wq_hi_ref[...], x_hi) + _dot(wq_hi_ref[...], x_lo)
                  + _dot(wq_lo_ref[...], x_hi))
    key_iota = lax.broadcasted_iota(jnp.int32, (PEER_N_KEYS, tile), 0).astype(F32)
    cpos = cpos_ref[...]

    def head_group(hg, carry):
        for g in range(2 * ROUTE_HEADS):
            i = 2 * ROUTE_HEADS * hg + g
            qs = q_scr[pl.ds(pl.multiple_of(i * PEER_KEY_HALF, PEER_KEY_HALF), PEER_KEY_HALF), :]
            s_scr[g] = jnp.dot(keys_ref[i], qs, precision=lax.Precision.HIGHEST,
                               preferred_element_type=F32)
        _extract_top(s_scr[...], key_iota, v_scr, i_scr)
        for hh in range(ROUTE_HEADS):
            v2_16 = v_scr[2 * hh + 1]
            v2_8 = v_scr[2 * hh + 1, 0:8, :]
            c_scr[hh] = jnp.concatenate(
                [v_scr[2 * hh, j:j + 1, :] + (v2_16 if j == 0 else v2_8) for j in range(PEER_TOPK)],
                axis=0) + cneg_ref[...]
        left = _extract_top(c_scr[...], cpos, None, None)
        for hh in range(ROUTE_HEADS):
            h = ROUTE_HEADS * hg + hh
            max1 = v_scr[2 * hh, 0:1, :]
            max2 = v_scr[2 * hh + 1, 0:1, :]
            cand = c_scr[hh]
            picked = jnp.where((left[hh] == NEG_INF) & (cand > NEG_INF), 1.0, 0.0)
            z = jnp.sum(picked * jnp.exp(cand - (max1 + max2)), axis=0, keepdims=True)
            nsel = jnp.zeros((PEER_N_KEYS, tile), F32)
            rank2 = jnp.full((PEER_N_KEYS, tile), float(PEER_TOPK), F32)
            row = 0
            for j in range(PEER_TOPK):
                n_j = jnp.sum(picked[row:row + CAND_ROWS[j], :], axis=0, keepdims=True)
                nsel = jnp.where(key_iota == i_scr[2 * hh, j:j + 1, :], n_j, nsel)
                rank2 = jnp.where(key_iota == i_scr[2 * hh + 1, j:j + 1, :], float(j), rank2)
                row += CAND_ROWS[j]
            rank2_ref[h] = rank2.astype(rank2_ref.dtype)
            nsel_ref[h] = nsel
            e2_ref[h] = jnp.exp(s_scr[2 * hh + 1] - max2).astype(e2_ref.dtype)
            c1_ref[h] = jnp.exp(s_scr[2 * hh] - max1) / z
        return carry

    lax.fori_loop(0, PEER_HEADS // ROUTE_HEADS, head_group, 0)


def _cand_constants(tile):
    pos, neg = [], []
    for j in range(PEER_TOPK):
        for l in range(CAND_ROWS[j]):
            pos.append(16.0 * j + l)
            neg.append(0.0 if l < CAND_COUNT[j] else NEG_INF)
    pos = np.broadcast_to(np.asarray(pos, np.float32)[:, None], (CAND_TOTAL, tile))
    neg = np.broadcast_to(np.asarray(neg, np.float32)[:, None], (CAND_TOTAL, tile))
    return jnp.asarray(pos), jnp.asarray(neg)


def peer_route(h, g, wq_t_hi, wq_t_lo, keys):
    tp = h.shape[0]
    cpos, cneg = _cand_constants(ROUTE_TILE)
    dense = jax.ShapeDtypeStruct((PEER_HEADS, PEER_N_KEYS, tp), F32)
    dense_bf = jax.ShapeDtypeStruct((PEER_HEADS, PEER_N_KEYS, tp), BF16)
    dense_spec = pl.BlockSpec((PEER_HEADS, PEER_N_KEYS, ROUTE_TILE), lambda i: (0, 0, i))
    return pl.pallas_call(
        _peer_route_kernel,
        grid=(tp // ROUTE_TILE,),
        in_specs=[_row_spec(ROUTE_TILE, D_MODEL), _full_spec((1, D_MODEL)),
                  _full_spec((D_MODEL, D_MODEL)), _full_spec((D_MODEL, D_MODEL)),
                  _full_spec((2 * PEER_HEADS, PEER_N_KEYS, PEER_KEY_HALF)),
                  _full_spec((CAND_TOTAL, ROUTE_TILE)), _full_spec((CAND_TOTAL, ROUTE_TILE))],
        out_specs=[pl.BlockSpec((D_MODEL, ROUTE_TILE), lambda i: (0, i)),
                   dense_spec, dense_spec, dense_spec, dense_spec],
        out_shape=[jax.ShapeDtypeStruct((D_MODEL, tp), BF16), dense_bf, dense_bf, dense, dense],
        scratch_shapes=[pltpu.VMEM((D_MODEL, ROUTE_TILE), F32),
                        pltpu.VMEM((2 * ROUTE_HEADS, PEER_N_KEYS, ROUTE_TILE), F32),
                        pltpu.VMEM((2 * ROUTE_HEADS, PEER_TOPK, ROUTE_TILE), F32),
                        pltpu.VMEM((2 * ROUTE_HEADS, PEER_TOPK, ROUTE_TILE), F32),
                        pltpu.VMEM((ROUTE_HEADS, CAND_TOTAL, ROUTE_TILE), F32)],
        compiler_params=_params("parallel"),
        name="peer_route",
    )(h, g, wq_t_hi, wq_t_lo, keys, cpos, cneg)


def _peer_dense_kernel(xt_ref, u_ref, v_ref, rank2_in, e2_in, nsel_ref, c1_ref, h_ref,
                       o_ref, act_scr, g_scr, w_scr, acc_scr, rank2_ref, e2_ref):
    j = pl.program_id(1)

    @pl.when(j == 0)
    def _():
        acc_scr[...] = jnp.zeros(acc_scr.shape, F32)
        rank2_ref[...] = rank2_in[...]
        e2_ref[...] = e2_in[...]

    pack = BF16_ROWS
    zero = jnp.zeros((1, LANES), BF16)
    n_cols = xt_ref.shape[1] // LANES
    a_per_chunk = EXPERT_CHUNK // PEER_N_KEYS
    for chunk in range(EXPERT_TILE // EXPERT_CHUNK):
        experts = slice(chunk * EXPERT_CHUNK, (chunk + 1) * EXPERT_CHUNK)
        for a_local in range(chunk * a_per_chunk, (chunk + 1) * a_per_chunk):
            a_row = slice(a_local, a_local + 1)
            rows = slice(a_local * PEER_N_KEYS, (a_local + 1) * PEER_N_KEYS)
            for col in range(n_cols):
                cols = slice(col * LANES, (col + 1) * LANES)
                gate = None
                for h in range(PEER_HEADS):
                    n_row = jnp.broadcast_to(nsel_ref[h, a_row, cols], (pack, LANES)).astype(BF16)
                    c_row = jnp.broadcast_to(c1_ref[h, a_row, cols], (pack, LANES)).astype(BF16)
                    n_all = jnp.concatenate([n_row] * (PEER_N_KEYS // pack), axis=0)
                    c_all = jnp.concatenate([c_row] * (PEER_N_KEYS // pack), axis=0)
                    term = jnp.where(rank2_ref[h, :, cols] < n_all, e2_ref[h, :, cols] * c_all, zero)
                    gate = term if gate is None else gate + term
                g_scr[rows, cols] = gate
        act_scr[experts, :] = _dot(u_ref[experts, :], xt_ref[...])
        for a_local in range(chunk * a_per_chunk, (chunk + 1) * a_per_chunk):
            rows = slice(a_local * PEER_N_KEYS, (a_local + 1) * PEER_N_KEYS)
            for col in range(n_cols):
                cols = slice(col * LANES, (col + 1) * LANES)
                act = act_scr[rows, cols]
                gelu = 0.5 * act * (1.0 + lax.erf(act * (1.0 / math.sqrt(2.0))))
                w_scr[rows, cols] = gelu.astype(BF16) * g_scr[rows, cols]
        acc_scr[...] += _dot(v_ref[:, experts], w_scr[experts, :])

    @pl.when(j == pl.num_programs(1) - 1)
    def _():
        o_ref[...] = h_ref[...] + acc_scr[...].T


def peer_dense(h, xt, u_bf, v_bf, rank2, e2, nsel, c1):
    tp = h.shape[0]
    n_exp = u_bf.shape[0]
    dense_spec = pl.BlockSpec((PEER_HEADS, PEER_N_KEYS, TOKEN_TILE), lambda i, j: (0, 0, i))
    a_spec = pl.BlockSpec((PEER_HEADS, EXPERT_TILE // PEER_N_KEYS, TOKEN_TILE),
                          lambda i, j: (0, j, i))
    return pl.pallas_call(
        _peer_dense_kernel,
        grid=(tp // TOKEN_TILE, n_exp // EXPERT_TILE),
        in_specs=[pl.BlockSpec((D_MODEL, TOKEN_TILE), lambda i, j: (0, i)),
                  pl.BlockSpec((EXPERT_TILE, D_MODEL), lambda i, j: (j, 0)),
                  pl.BlockSpec((D_MODEL, EXPERT_TILE), lambda i, j: (0, j)),
                  dense_spec, dense_spec, a_spec, a_spec,
                  pl.BlockSpec((TOKEN_TILE, D_MODEL), lambda i, j: (i, 0))],
        out_specs=pl.BlockSpec((TOKEN_TILE, D_MODEL), lambda i, j: (i, 0)),
        out_shape=jax.ShapeDtypeStruct((tp, D_MODEL), F32),
        scratch_shapes=[pltpu.VMEM((EXPERT_TILE, TOKEN_TILE), F32),
                        pltpu.VMEM((EXPERT_TILE, TOKEN_TILE), BF16),
                        pltpu.VMEM((EXPERT_TILE, TOKEN_TILE), BF16),
                        pltpu.VMEM((D_MODEL, TOKEN_TILE), F32),
                        pltpu.VMEM((PEER_HEADS, PEER_N_KEYS, TOKEN_TILE), BF16),
                        pltpu.VMEM((PEER_HEADS, PEER_N_KEYS, TOKEN_TILE), BF16)],
        compiler_params=_params("parallel", "arbitrary"),
        name="peer_dense",
    )(xt, u_bf, v_bf, rank2, e2, nsel, c1, h)


def peer_layer(h, g, wq, sub_keys, u, v):
    wq_t_hi, wq_t_lo = _split(wq.T)
    keys = sub_keys.reshape(2 * PEER_HEADS, PEER_N_KEYS, PEER_KEY_HALF)
    xt, rank2, e2, nsel, c1 = peer_route(h, g, wq_t_hi, wq_t_lo, keys)
    return peer_dense(h, xt, u.astype(BF16), v.astype(BF16).T, rank2, e2, nsel, c1)


def _bucket_starts():
    max_exact = T5_BUCKETS // 2
    d = np.arange(0, 4 * T5_MAX_DIST)
    df = np.maximum(d, 1).astype(np.float64)
    large = max_exact + (np.log(df / max_exact) / math.log(T5_MAX_DIST / max_exact)
                         * (T5_BUCKETS - max_exact)).astype(np.int64)
    bucket = np.where(d < max_exact, d, np.minimum(large, T5_BUCKETS - 1))
    starts = []
    for b in range(T5_BUCKETS):
        idx = np.nonzero(bucket == b)[0]
        if idx.size:
            starts.append((int(idx[0]), b))
    return starts


BUCKET_STARTS = _bucket_starts()


def _bias_from_dist(dist, table):
    out = None
    for start, b in BUCKET_STARTS:
        val = table(b)
        out = val + jnp.zeros(dist.shape, F32) if out is None else jnp.where(dist >= start, val, out)
    return out


def _moba_prompt_kernel(rb_ref, q_ref, k_ref, v_ref, o_ref,
                        kb_scr, vt_scr, km_scr, bias_scr, sel_scr, m_scr, l_scr, acc_scr):
    pair = pl.program_id(1)
    qt = pl.program_id(2)
    n_blk = k_ref.shape[0] // MOBA_BLOCK
    blk = MOBA_BLOCK
    lane = lax.broadcasted_iota(jnp.int32, (1, 2 * HEAD_DIM), 1)

    @pl.when(qt == 0)
    def _():
        for b in range(n_blk):
            kt = k_ref[b * blk:(b + 1) * blk, :]
            kb_scr[b] = kt.astype(BF16)
            km_scr[b:b + 1, :] = jnp.mean(kt, axis=0, keepdims=True)
            vt_scr[b] = v_ref[b * blk:(b + 1) * blk, :].T.astype(BF16)
        ik = lax.broadcasted_iota(jnp.int32, (blk, blk), 0)
        iq = lax.broadcasted_iota(jnp.int32, (blk, blk), 1)
        for hh in range(2):
            head = 2 * pair + hh
            for delta in range(2):
                dist = iq - ik + delta * blk
                bias = _bias_from_dist(jnp.maximum(dist, 0), lambda b: rb_ref[b, head])
                bias_scr[hh, delta] = jnp.where(dist >= 0, bias, NEG_INF)

    own = qt
    blk_iota = lax.broadcasted_iota(jnp.int32, (n_blk, blk), 0)
    heads = range(2)
    qh_bf, far_bias = [], []
    for hh in heads:
        head = 2 * pair + hh
        in_head = (lane >= hh * HEAD_DIM) & (lane < (hh + 1) * HEAD_DIM)
        qh = jnp.where(in_head, q_ref[...] * (HEAD_DIM ** -0.5), 0.0)
        qh_bf.append(qh.astype(BF16))
        km = jnp.where(in_head, km_scr[...], 0.0)
        gate = lax.dot_general(km, qh, (((1,), (1,)), ((), ())),
                               precision=lax.Precision.HIGHEST,
                               preferred_element_type=F32)
        gate = jnp.where(blk_iota < own, gate, NEG_INF)
        rank = jnp.zeros((n_blk, blk), F32)
        for b in range(n_blk):
            gb = gate[b:b + 1, :]
            ahead = (gb > gate) | ((gb == gate) & (b < blk_iota))
            rank = rank + jnp.where(ahead, 1.0, 0.0)
        sel_scr[hh] = jnp.where((blk_iota < own) & (rank < float(MOBA_TOPK)), 1.0, 0.0)
        far_bias.append(rb_ref[T5_BUCKETS - 1, head])

    def step(hh, blocks, first):
        v_rows = slice(hh * HEAD_DIM, (hh + 1) * HEAD_DIM)
        scores = []
        for kb, bias in blocks:
            s = _dot_nt(kb_scr[kb], qh_bf[hh])
            if first:
                s = s + bias
            else:
                s = s + (far_bias[hh] if bias is None else bias)
                s = jnp.where(sel_scr[hh, pl.ds(kb, 1), :] > 0.0, s, NEG_INF)
            scores.append(s)
        m_new = None if first else m_scr[hh]
        for s in scores:
            m_blk = jnp.max(s, axis=0, keepdims=True)
            m_new = m_blk if m_new is None else jnp.maximum(m_new, m_blk)
        pv, p_sum = None, None
        for (kb, _), s in zip(blocks, scores):
            p = jnp.exp(s - m_new)
            part = _dot(vt_scr[kb, v_rows, :], p.astype(BF16))
            part_sum = jnp.sum(p, axis=0, keepdims=True)
            pv = part if pv is None else pv + part
            p_sum = part_sum if p_sum is None else p_sum + part_sum
        if first:
            l_scr[hh] = p_sum
            acc_scr[v_rows, :] = pv
        else:
            alpha = jnp.exp(m_scr[hh] - m_new)
            l_scr[hh] = alpha * l_scr[hh] + p_sum
            acc_scr[v_rows, :] = alpha * acc_scr[v_rows, :] + pv
        m_scr[hh] = m_new

    for hh in heads:
        step(hh, [(own, bias_scr[hh, 0])], True)

    @pl.when(own >= 1)
    def _():
        for hh in heads:
            step(hh, [(own - 1, bias_scr[hh, 1])], False)

    n_far = jnp.maximum(own - 1, 0)

    def far_pair(i, carry):
        for hh in heads:
            step(hh, [(2 * i, None), (2 * i + 1, None)], False)
        return carry

    lax.fori_loop(0, n_far // 2, far_pair, 0)

    @pl.when(n_far % 2 == 1)
    def _():
        for hh in heads:
            step(hh, [(n_far - 1, None)], False)
    for hh in heads:
        v_rows = slice(hh * HEAD_DIM, (hh + 1) * HEAD_DIM)
        acc_scr[v_rows, :] = acc_scr[v_rows, :] / l_scr[hh]
    o_ref[...] = acc_scr[...].T


def moba_prompt(q, k, v, rel_bias, n_b, seq):
    assert seq % MOBA_BLOCK == 0
    nq = seq // MOBA_BLOCK
    width = 2 * HEAD_DIM
    grid_spec = pltpu.PrefetchScalarGridSpec(
        num_scalar_prefetch=0,
        grid=(n_b, N_HEADS // 2, nq),
        in_specs=[pl.BlockSpec(memory_space=pltpu.SMEM),
                  pl.BlockSpec((MOBA_BLOCK, width), lambda n, p, t: (n * nq + t, p)),
                  pl.BlockSpec((seq, width), lambda n, p, t: (n, p)),
                  pl.BlockSpec((seq, width), lambda n, p, t: (n, p))],
        out_specs=pl.BlockSpec((MOBA_BLOCK, width), lambda n, p, t: (n * nq + t, p)),
        scratch_shapes=[pltpu.VMEM((nq, MOBA_BLOCK, width), BF16),
                        pltpu.VMEM((nq, width, MOBA_BLOCK), BF16),
                        pltpu.VMEM((nq, width), F32),
                        pltpu.VMEM((2, 2, MOBA_BLOCK, MOBA_BLOCK), F32),
                        pltpu.VMEM((2, nq, MOBA_BLOCK), F32),
                        pltpu.VMEM((2, 1, MOBA_BLOCK), F32),
                        pltpu.VMEM((2, 1, MOBA_BLOCK), F32),
                        pltpu.VMEM((width, MOBA_BLOCK), F32)],
    )
    return pl.pallas_call(
        _moba_prompt_kernel,
        grid_spec=grid_spec,
        out_shape=jax.ShapeDtypeStruct((n_b * seq, D_MODEL), F32),
        compiler_params=_params("parallel", "parallel", "arbitrary"),
        name="moba_prompt",
    )(rel_bias, q, k, v)


def _moba_sample_kernel(pt_ref, rbt_ref, q_ref, kn_ref, vn_ref, *refs, n_pages):
    k_refs, v_refs, o_ref = refs[:n_pages], refs[n_pages:2 * n_pages], refs[2 * n_pages]
    blk = MOBA_BLOCK
    pages_per_blk = blk // PAGE_SIZE
    n_blk = n_pages // pages_per_blk
    stat = (N_HEADS, STAT_LANES)
    head_of_lane = lax.broadcasted_iota(jnp.int32, (N_HEADS, D_MODEL), 1) // HEAD_DIM
    head_of_row = lax.broadcasted_iota(jnp.int32, (N_HEADS, D_MODEL), 0)
    diag = head_of_lane == head_of_row
    qbd = jnp.where(diag, q_ref[0] * (HEAD_DIM ** -0.5), 0.0)
    q_hi, q_lo = _split(qbd)
    q_hilo = jnp.concatenate([q_hi, q_lo], axis=0)
    key_iota = lax.broadcasted_iota(jnp.int32, (N_HEADS, blk), 1)

    gates, maxes, sums, outs = [], [], [], []
    for b in range(n_blk):
        pages = range(pages_per_blk * b, pages_per_blk * (b + 1))
        raw = [_dot(q_hilo, k_refs[i][0].astype(BF16)) for i in pages]
        s = jnp.concatenate([r[:N_HEADS] + r[N_HEADS:] for r in raw], axis=1)
        gate = jnp.mean(s, axis=1, keepdims=True)
        dist = (n_blk - b) * blk - key_iota
        s = s + _bias_from_dist(dist, lambda t: rbt_ref[:, t:t + 1])
        m = jnp.max(s, axis=1, keepdims=True)
        p = jnp.exp(s - m).astype(BF16)
        out = None
        for idx, i in enumerate(pages):
            part = _dot_nt(p[:, idx * PAGE_SIZE:(idx + 1) * PAGE_SIZE], v_refs[i][0].astype(BF16))
            out = part if out is None else out + part
        gates.append(jnp.broadcast_to(gate, stat))
        maxes.append(jnp.broadcast_to(m, stat))
        sums.append(jnp.broadcast_to(jnp.sum(p.astype(F32), axis=1, keepdims=True), stat))
        outs.append(out)

    s_self = jnp.sum(qbd * kn_ref[0], axis=1, keepdims=True) + rbt_ref[:, 0:1]
    m_tot = jnp.broadcast_to(s_self, stat)
    keep = []
    for i in range(n_blk):
        rank = jnp.zeros(stat, F32)
        for i2 in range(n_blk):
            if i2 == i:
                continue
            ahead = (gates[i2] > gates[i]) if i2 > i else (gates[i2] >= gates[i])
            rank = rank + jnp.where(ahead, 1.0, 0.0)
        keep.append(rank < float(MOBA_TOPK))
        m_tot = jnp.where(keep[i], jnp.maximum(m_tot, maxes[i]), m_tot)
    w_self = jnp.exp(s_self - m_tot)
    l_tot = w_self
    out = w_self[:, 0:1] * vn_ref[0]
    for i in range(n_blk):
        w_i = jnp.where(keep[i], jnp.exp(maxes[i] - m_tot), 0.0)
        l_tot = l_tot + w_i * sums[i]
        out = out + w_i[:, 0:1] * outs[i]
    out = jnp.where(diag, out / l_tot[:, 0:1], 0.0)
    o_ref[0] = jnp.sum(out, axis=0, keepdims=True)


def moba_sample(q, k_new, v_new, cache_k, cache_v, page_table, rel_bias):
    n_d, n_pages = page_table.shape
    assert n_pages % (MOBA_BLOCK // PAGE_SIZE) == 0
    n_phys = cache_k.shape[0]
    ck = cache_k.transpose(0, 2, 3, 1).reshape(n_phys, D_MODEL, PAGE_SIZE)
    cv = cache_v.transpose(0, 2, 3, 1).reshape(n_phys, D_MODEL, PAGE_SIZE)
    row = pl.BlockSpec((1, 1, D_MODEL), lambda n, pt: (n, 0, 0))

    def page(i):
        return pl.BlockSpec((1, D_MODEL, PAGE_SIZE), lambda n, pt: (pt[n, i], 0, 0))

    pages = [page(i) for i in range(n_pages)]
    grid_spec = pltpu.PrefetchScalarGridSpec(
        num_scalar_prefetch=1,
        grid=(n_d,),
        in_specs=[pl.BlockSpec((N_HEADS, T5_BUCKETS), lambda n, pt: (0, 0)),
                  row, row, row] + pages + pages,
        out_specs=row,
    )
    out = pl.pallas_call(
        functools.partial(_moba_sample_kernel, n_pages=n_pages),
        grid_spec=grid_spec,
        out_shape=jax.ShapeDtypeStruct((n_d, 1, D_MODEL), F32),
        compiler_params=_params("parallel"),
        name="moba_sample",
    )(page_table, rel_bias.T, q[:, None, :], k_new[:, None, :], v_new[:, None, :],
      *([ck] * n_pages), *([cv] * n_pages))
    return out[:, 0, :]


def kernel(x_prompt, x_sample, state_conv, cache_k, cache_v, page_table, p_prompt, p_sample, rel_bias, norm_mix_g, norm_ffn_g, norm_ple_g, norm_final_g, conv_w_in, conv_b_in, conv_dw_w, conv_dw_b, conv_ln_g, conv_ln_b, conv_w_out, conv_b_out, attn_w_qkv, attn_w_o, peer_w_q, peer_sub_keys, peer_u, peer_v, ple_w_proj, ple_w_gate):
    n_b, seq, d = x_prompt.shape
    n_d, dec_seq, _ = x_sample.shape
    assert d == D_MODEL and dec_seq == 1 and seq % TOKEN_TILE == 0
    depth = norm_mix_g.shape[0]
    n_p = n_b * seq
    tp = -(-(n_p + n_d) // TOKEN_TILE) * TOKEN_TILE
    pad = tp - n_p - n_d

    def pack(prompt_rows, sample_rows):
        tail = jnp.zeros((pad, prompt_rows.shape[1]), prompt_rows.dtype)
        return jnp.concatenate([prompt_rows, sample_rows.astype(prompt_rows.dtype), tail], axis=0)

    def row(vec):
        return vec.reshape(1, -1)

    h = pack(x_prompt.reshape(n_p, d), x_sample.reshape(n_d, d))
    conv_p, conv_s, kp, vp, ks, vs = [], [], [], [], [], []
    zero_bias = jnp.zeros((1, d), F32)
    for i in range(depth):
        j = i // 2
        if i % 2 == 0:
            glu = glu_in(h, row(norm_mix_g[i]), conv_w_in[j].astype(BF16), row(conv_b_in[j]))
            dwk = jnp.concatenate([conv_dw_w[j][::-1], jnp.zeros((1, d), F32)], axis=0)
            y_p, tail = conv_prompt(glu, n_b, seq, dwk, row(conv_dw_b[j]),
                                    row(conv_ln_g[j]), row(conv_ln_b[j]))
            glu_s = glu[n_p:n_p + n_d]
            y_s = conv_sample(glu_s, state_conv[j], conv_dw_w[j], row(conv_dw_b[j]),
                              row(conv_ln_g[j]), row(conv_ln_b[j]))
            conv_p.append(tail[:, HALO - (CONV_WIDTH - 1):, :])
            conv_s.append(jnp.concatenate([state_conv[j][:, 1:, :], glu_s[:, None, :]], axis=1))
            h = proj_residual(h, pack(y_p, y_s), conv_w_out[j].astype(BF16), row(conv_b_out[j]))
        else:
            wq_hi, wq_lo = _split(attn_w_qkv[j][:, :d])
            q, k, v = qkv_proj(h, row(norm_mix_g[i]), wq_hi, wq_lo, attn_w_qkv[j][:, d:].astype(BF16))
            o_p = moba_prompt(q, k, v, rel_bias, n_b, seq)
            o_s = moba_sample(q[n_p:n_p + n_d], k[n_p:n_p + n_d], v[n_p:n_p + n_d],
                              cache_k[j], cache_v[j], page_table, rel_bias)
            kp.append(k[:n_p].reshape(n_b, seq, N_HEADS, HEAD_DIM))
            vp.append(v[:n_p].reshape(n_b, seq, N_HEADS, HEAD_DIM))
            ks.append(k[n_p:n_p + n_d].reshape(n_d, 1, N_HEADS, HEAD_DIM))
            vs.append(v[n_p:n_p + n_d].reshape(n_d, 1, N_HEADS, HEAD_DIM))
            h = proj_residual(h, pack(o_p, o_s), attn_w_o[j].astype(BF16), zero_bias)
        h = peer_layer(h, row(norm_ffn_g[i]), peer_w_q[i], peer_sub_keys[i], peer_u[i], peer_v[i])
        p_all = pack(p_prompt[i].reshape(n_p, -1), p_sample[i].reshape(n_d, -1))
        h = ple(h, p_all, row(norm_ple_g[i]), ple_w_gate[i].astype(BF16),
                ple_w_proj[i].astype(BF16), row(norm_final_g), final_norm=(i == depth - 1))
    y_prompt = h[:n_p].reshape(n_b, seq, d)
    y_sample = h[n_p:n_p + n_d].reshape(n_d, 1, d)
    return (y_prompt, y_sample, jnp.stack(conv_p), jnp.stack(conv_s),
            jnp.stack(kp), jnp.stack(vp), jnp.stack(ks), jnp.stack(vs))
```

```python
import functools
import math

import numpy as np
import jax
import jax.numpy as jnp
from jax import lax
from jax.experimental import pallas as pl
from jax.experimental.pallas import tpu as pltpu

F32 = jnp.float32
BF16 = jnp.bfloat16
EPS = 1e-6
NEG_INF = float("-inf")

D_MODEL = 1024
N_HEADS = 16
HEAD_DIM = 64
CONV_WIDTH = 31
MOBA_BLOCK = 256
MOBA_TOPK = 3
T5_BUCKETS = 32
T5_MAX_DIST = 128
PEER_HEADS = 8
PEER_N_KEYS = 128
PEER_KEY_HALF = 64
PEER_TOPK = 16
PAGE_SIZE = 128

TOKEN_TILE = 512
ROUTE_TILE = 256
ROUTE_HEADS = 2
EXPERT_TILE = 2048
EXPERT_CHUNK = 2048
CONV_CHUNK = 64
HALO = 32
STAT_LANES = 128
LANES = 128
BF16_ROWS = 16
VMEM_LIMIT = 56 * 1024 * 1024

CAND_COUNT = [PEER_TOPK // (j + 1) for j in range(PEER_TOPK)]
CAND_PIECES = [
    [(0, 8, 0, 0)], [(0, 8, 0, 8)], [(1, 8, 0, 0)],
    [(2, 5, 0, 0), (5, 2, 5, 0)],
    [(3, 4, 0, 0), (4, 3, 4, 0)],
    [(6, 2, 0, 0), (7, 2, 2, 0), (8, 1, 4, 0), (9, 1, 5, 0), (10, 1, 6, 0), (11, 1, 7, 0)],
    [(12, 1, 0, 0), (13, 1, 1, 0), (14, 1, 2, 0), (15, 1, 3, 0)],
]
CAND_TOTAL = 8 * len(CAND_PIECES)
CAND_PAD_POS = 1e6
assert all(sum(c for j, c, _, _ in sum(CAND_PIECES, []) if j == jj) == CAND_COUNT[jj]
           for jj in range(PEER_TOPK))


def _params(*sem):
    return pltpu.CompilerParams(dimension_semantics=sem, vmem_limit_bytes=VMEM_LIMIT)


def _dot(a, b):
    return jnp.dot(a, b, preferred_element_type=F32)


def _dot_nt(a, b):
    return lax.dot_general(a, b, (((1,), (1,)), ((), ())), preferred_element_type=F32)


def _split(x):
    hi = x.astype(BF16)
    lo = (x - hi.astype(F32)).astype(BF16)
    return hi, lo


def _dot3(a_hi, a_lo, b_hi, b_lo):
    return _dot(a_hi, b_hi) + _dot(a_hi, b_lo) + _dot(a_lo, b_hi)


def _rms(x, g):
    return x * lax.rsqrt(jnp.mean(x * x, axis=-1, keepdims=True) + EPS) * g


def _sigmoid(x):
    return 1.0 / (1.0 + jnp.exp(-x))


def _row_spec(tile, width):
    return pl.BlockSpec((tile, width), lambda i: (i, 0))


def _full_spec(shape):
    nd = len(shape)
    return pl.BlockSpec(shape, lambda *_: (0,) * nd)


def _glu_in_kernel(h_ref, g_ref, w_ref, b_ref, o_ref):
    a = _rms(h_ref[...], g_ref[...]).astype(BF16)
    z = _dot(a, w_ref[...]) + b_ref[...]
    o_ref[...] = z[:, :D_MODEL] * _sigmoid(z[:, D_MODEL:])


def glu_in(h, g, w_bf, b):
    tp = h.shape[0]
    return pl.pallas_call(
        _glu_in_kernel,
        grid=(tp // TOKEN_TILE,),
        in_specs=[_row_spec(TOKEN_TILE, D_MODEL), _full_spec((1, D_MODEL)),
                  _full_spec((D_MODEL, 2 * D_MODEL)), _full_spec((1, 2 * D_MODEL))],
        out_specs=_row_spec(TOKEN_TILE, D_MODEL),
        out_shape=jax.ShapeDtypeStruct((tp, D_MODEL), F32),
        compiler_params=_params("parallel"),
        name="glu_in",
    )(h, g, w_bf, b)


def _proj_residual_kernel(h_ref, y_ref, w_ref, b_ref, o_ref):
    o_ref[...] = h_ref[...] + _dot(y_ref[...].astype(BF16), w_ref[...]) + b_ref[...]


def proj_residual(h, y, w_bf, b):
    tp = h.shape[0]
    return pl.pallas_call(
        _proj_residual_kernel,
        grid=(tp // TOKEN_TILE,),
        in_specs=[_row_spec(TOKEN_TILE, D_MODEL), _row_spec(TOKEN_TILE, D_MODEL),
                  _full_spec((D_MODEL, D_MODEL)), _full_spec((1, D_MODEL))],
        out_specs=_row_spec(TOKEN_TILE, D_MODEL),
        out_shape=jax.ShapeDtypeStruct((tp, D_MODEL), F32),
        compiler_params=_params("parallel"),
        name="proj_residual",
    )(h, y, w_bf, b)


def _ple_kernel(h_ref, p_ref, g_ref, wg_ref, wp_ref, gf_ref, o_ref, *, final_norm):
    h = h_ref[...]
    gate = _sigmoid(_dot(_rms(h, g_ref[...]).astype(BF16), wg_ref[...]))
    out = h + gate * _dot(p_ref[...].astype(BF16), wp_ref[...])
    if final_norm:
        out = _rms(out, gf_ref[...])
    o_ref[...] = out


def ple(h, p, g, wg_bf, wp_bf, g_final, final_norm):
    tp = h.shape[0]
    pdim = p.shape[1]
    return pl.pallas_call(
        functools.partial(_ple_kernel, final_norm=final_norm),
        grid=(tp // TOKEN_TILE,),
        in_specs=[_row_spec(TOKEN_TILE, D_MODEL), _row_spec(TOKEN_TILE, pdim),
                  _full_spec((1, D_MODEL)), _full_spec((D_MODEL, D_MODEL)),
                  _full_spec((pdim, D_MODEL)), _full_spec((1, D_MODEL))],
        out_specs=_row_spec(TOKEN_TILE, D_MODEL),
        out_shape=jax.ShapeDtypeStruct((tp, D_MODEL), F32),
        compiler_params=_params("parallel"),
        name="ple_final" if final_norm else "ple",
    )(h, p, g, wg_bf, wp_bf, g_final)


def _qkv_kernel(h_ref, g_ref, wq_hi_ref, wq_lo_ref, wkv_ref, q_ref, k_ref, v_ref):
    a_hi, a_lo = _split(_rms(h_ref[...], g_ref[...]))
    q_ref[...] = _dot3(a_hi, a_lo, wq_hi_ref[...], wq_lo_ref[...])
    kv = _dot(a_hi, wkv_ref[...])
    k_ref[...] = kv[:, :D_MODEL]
    v_ref[...] = kv[:, D_MODEL:]


def qkv_proj(h, g, wq_hi, wq_lo, wkv_bf):
    tp = h.shape[0]
    out = jax.ShapeDtypeStruct((tp, D_MODEL), F32)
    return pl.pallas_call(
        _qkv_kernel,
        grid=(tp // TOKEN_TILE,),
        in_specs=[_row_spec(TOKEN_TILE, D_MODEL), _full_spec((1, D_MODEL)),
                  _full_spec((D_MODEL, D_MODEL)), _full_spec((D_MODEL, D_MODEL)),
                  _full_spec((D_MODEL, 2 * D_MODEL))],
        out_specs=[_row_spec(TOKEN_TILE, D_MODEL)] * 3,
        out_shape=[out, out, out],
        compiler_params=_params("parallel"),
        name="qkv_proj",
    )(h, g, wq_hi, wq_lo, wkv_bf)


def _ln_silu(y, g, b):
    mu = jnp.mean(y, axis=-1, keepdims=True)
    yc = y - mu
    y = yc * lax.rsqrt(jnp.mean(yc * yc, axis=-1, keepdims=True) + EPS) * g + b
    return y * _sigmoid(y)


def _conv_prompt_kernel(x_ref, dwk_ref, dwb_ref, lng_ref, lnb_ref, y_ref, st_ref, buf_ref):
    tile = x_ref.shape[0]

    @pl.when(pl.program_id(1) == 0)
    def _():
        buf_ref[0:HALO, :] = jnp.zeros((HALO, D_MODEL), F32)

    buf_ref[HALO:HALO + tile, :] = x_ref[...]

    def chunk(c, carry):
        t0 = pl.multiple_of(c * CONV_CHUNK, CONV_CHUNK)
        win = buf_ref[pl.ds(t0, CONV_CHUNK + HALO), :]
        acc = None
        for s in range(8):
            part = None
            for u in range(4):
                k = 8 * u + s
                term = win[24 - 8 * u:24 - 8 * u + CONV_CHUNK + 8, :] * dwk_ref[k:k + 1, :]
                part = term if part is None else part + term
            if s:
                part = pltpu.roll(part, s, 0)
            part = part[8:8 + CONV_CHUNK, :]
            acc = part if acc is None else acc + part
        y = _ln_silu(acc + dwb_ref[...], lng_ref[...], lnb_ref[...])
        y_ref[pl.ds(t0, CONV_CHUNK), :] = y.astype(y_ref.dtype)
        return carry

    lax.fori_loop(0, tile // CONV_CHUNK, chunk, 0)
    tail = buf_ref[tile:tile + HALO, :]
    st_ref[0] = tail
    buf_ref[0:HALO, :] = tail


def conv_prompt(glu, n_b, seq, dwk, dwb, lng, lnb):
    nt = seq // TOKEN_TILE
    return pl.pallas_call(
        _conv_prompt_kernel,
        grid=(n_b, nt),
        in_specs=[pl.BlockSpec((TOKEN_TILE, D_MODEL), lambda n, t: (n * nt + t, 0)),
                  _full_spec((32, D_MODEL)), _full_spec((1, D_MODEL)),
                  _full_spec((1, D_MODEL)), _full_spec((1, D_MODEL))],
        out_specs=[pl.BlockSpec((TOKEN_TILE, D_MODEL), lambda n, t: (n * nt + t, 0)),
                   pl.BlockSpec((1, HALO, D_MODEL), lambda n, t: (n, 0, 0))],
        out_shape=[jax.ShapeDtypeStruct((n_b * seq, D_MODEL), BF16),
                   jax.ShapeDtypeStruct((n_b, HALO, D_MODEL), F32)],
        scratch_shapes=[pltpu.VMEM((TOKEN_TILE + HALO, D_MODEL), F32)],
        compiler_params=_params("arbitrary", "arbitrary"),
        name="conv_prompt",
    )(glu, dwk, dwb, lng, lnb)


def _conv_sample_kernel(x_ref, st_ref, dw_ref, dwb_ref, lng_ref, lnb_ref, y_ref):
    hist = jnp.sum(st_ref[...] * dw_ref[0:CONV_WIDTH - 1, :][None], axis=1)
    y = hist + x_ref[...] * dw_ref[CONV_WIDTH - 1:CONV_WIDTH, :] + dwb_ref[...]
    y_ref[...] = _ln_silu(y, lng_ref[...], lnb_ref[...]).astype(y_ref.dtype)


def conv_sample(glu_s, state, dw, dwb, lng, lnb):
    n_d = glu_s.shape[0]
    blk = 32 if n_d % 32 == 0 else n_d
    return pl.pallas_call(
        _conv_sample_kernel,
        grid=(n_d // blk,),
        in_specs=[_row_spec(blk, D_MODEL),
                  pl.BlockSpec((blk, CONV_WIDTH - 1, D_MODEL), lambda i: (i, 0, 0)),
                  _full_spec((CONV_WIDTH, D_MODEL)), _full_spec((1, D_MODEL)),
                  _full_spec((1, D_MODEL)), _full_spec((1, D_MODEL))],
        out_specs=_row_spec(blk, D_MODEL),
        out_shape=jax.ShapeDtypeStruct((n_d, D_MODEL), BF16),
        compiler_params=_params("parallel"),
        name="conv_sample",
    )(glu_s, state, dw, dwb, lng, lnb)


def _extract_top(w, tie_key, vals_ref, first_ref):
    def body(r, w):
        m = jnp.max(w, axis=1, keepdims=True)
        first = jnp.min(jnp.where(w == m, tie_key, 1e9), axis=1, keepdims=True)
        for g in range(w.shape[0]):
            if vals_ref is not None:
                vals_ref[g, pl.ds(r, 1), :] = m[g]
            if first_ref is not None:
                first_ref[g, pl.ds(r, 1), :] = first[g]
        return jnp.where(tie_key == first, NEG_INF, w)

    return lax.fori_loop(0, PEER_TOPK, body, w)


def _peer_route_kernel(x_ref, g_ref, wq_hi_ref, wq_lo_ref, keys_ref, cpos_ref, cneg_ref,
                       xt_ref, rank2_ref, e2_ref, nsel_ref, c1_ref,
                       q_scr, s_scr, v_scr, i_scr, c_scr):
    tile = x_ref.shape[0]
    xn_t = _rms(x_ref[...], g_ref[...]).T
    x_hi, x_lo = _split(xn_t)
    xt_ref[...] = x_hi
    q_scr[...] = (_dot(wq_hi_ref[...], x_hi) + _dot(wq_hi_ref[...], x_lo)
                  + _dot(wq_lo_ref[...], x_hi))
    key_iota = lax.broadcasted_iota(jnp.int32, (PEER_N_KEYS, tile), 0).astype(F32)
    sub = lax.broadcasted_iota(jnp.int32, (8, tile), 0)
    cpos = cpos_ref[...]

    def head_group(hg, carry):
        for g in range(2 * ROUTE_HEADS):
            i = 2 * ROUTE_HEADS * hg + g
            qs = q_scr[pl.ds(pl.multiple_of(i * PEER_KEY_HALF, PEER_KEY_HALF), PEER_KEY_HALF), :]
            s_scr[g] = jnp.dot(keys_ref[i], qs, precision=lax.Precision.HIGHEST,
                               preferred_element_type=F32)
        _extract_top(s_scr[...], key_iota, v_scr, i_scr)
        for hh in range(ROUTE_HEADS):
            v2 = v_scr[2 * hh + 1]
            pieces = []
            for runs in CAND_PIECES:
                p1 = p2 = None
                for j, _, off, l0 in runs:
                    t1 = v_scr[2 * hh, j:j + 1, :]
                    t2 = v2[l0:l0 + 8] if off == 0 else pltpu.roll(v2[0:8], off, 0)
                    p1 = jnp.broadcast_to(t1, t2.shape) if p1 is None else jnp.where(sub >= off, t1, p1)
                    p2 = t2 if p2 is None else jnp.where(sub >= off, t2, p2)
                pieces.append(p1 + p2)
            c_scr[hh] = jnp.concatenate(pieces, axis=0) + cneg_ref[...]
        left = _extract_top(c_scr[...], cpos, None, None)
        for hh in range(ROUTE_HEADS):
            h = ROUTE_HEADS * hg + hh
            max1 = v_scr[2 * hh, 0:1, :]
            max2 = v_scr[2 * hh + 1, 0:1, :]
            cand = c_scr[hh]
            picked = jnp.where((left[hh] == NEG_INF) & (cand > NEG_INF), 1.0, 0.0)
            z = jnp.sum(picked * jnp.exp(cand - (max1 + max2)), axis=0, keepdims=True)
            n_sel = [None] * PEER_TOPK
            for pi, runs in enumerate(CAND_PIECES):
                piece = picked[8 * pi:8 * pi + 8, :]
                for j, cnt, off, _ in runs:
                    part = piece if cnt == 8 else jnp.where((sub >= off) & (sub < off + cnt), piece, 0.0)
                    part = jnp.sum(part, axis=0, keepdims=True)
                    n_sel[j] = part if n_sel[j] is None else n_sel[j] + part
            nsel = jnp.zeros((PEER_N_KEYS, tile), F32)
            rank2 = jnp.full((PEER_N_KEYS, tile), float(PEER_TOPK), F32)
            for j in range(PEER_TOPK):
                nsel = jnp.where(key_iota == i_scr[2 * hh, j:j + 1, :], n_sel[j], nsel)
                rank2 = jnp.where(key_iota == i_scr[2 * hh + 1, j:j + 1, :], float(j), rank2)
            rank2_ref[h] = rank2.astype(rank2_ref.dtype)
            nsel_ref[h] = nsel
            e2_ref[h] = jnp.exp(s_scr[2 * hh + 1] - max2).astype(e2_ref.dtype)
            c1_ref[h] = jnp.exp(s_scr[2 * hh] - max1) / z
        return carry

    lax.fori_loop(0, PEER_HEADS // ROUTE_HEADS, head_group, 0)


def _cand_constants(tile):
    pos, neg = [], []
    for runs in CAND_PIECES:
        piece_pos, piece_neg = [CAND_PAD_POS] * 8, [NEG_INF] * 8
        for j, cnt, off, l0 in runs:
            for r in range(cnt):
                piece_pos[off + r] = 16.0 * j + l0 + r
                piece_neg[off + r] = 0.0
        pos += piece_pos
        neg += piece_neg
    pos = np.broadcast_to(np.asarray(pos, np.float32)[:, None], (CAND_TOTAL, tile))
    neg = np.broadcast_to(np.asarray(neg, np.float32)[:, None], (CAND_TOTAL, tile))
    return jnp.asarray(pos), jnp.asarray(neg)


def peer_route(h, g, wq_t_hi, wq_t_lo, keys):
    tp = h.shape[0]
    cpos, cneg = _cand_constants(ROUTE_TILE)
    dense = jax.ShapeDtypeStruct((PEER_HEADS, PEER_N_KEYS, tp), F32)
    dense_bf = jax.ShapeDtypeStruct((PEER_HEADS, PEER_N_KEYS, tp), BF16)
    dense_spec = pl.BlockSpec((PEER_HEADS, PEER_N_KEYS, ROUTE_TILE), lambda i: (0, 0, i))
    return pl.pallas_call(
        _peer_route_kernel,
        grid=(tp // ROUTE_TILE,),
        in_specs=[_row_spec(ROUTE_TILE, D_MODEL), _full_spec((1, D_MODEL)),
                  _full_spec((D_MODEL, D_MODEL)), _full_spec((D_MODEL, D_MODEL)),
                  _full_spec((2 * PEER_HEADS, PEER_N_KEYS, PEER_KEY_HALF)),
                  _full_spec((CAND_TOTAL, ROUTE_TILE)), _full_spec((CAND_TOTAL, ROUTE_TILE))],
        out_specs=[pl.BlockSpec((D_MODEL, ROUTE_TILE), lambda i: (0, i)),
                   dense_spec, dense_spec, dense_spec, dense_spec],
        out_shape=[jax.ShapeDtypeStruct((D_MODEL, tp), BF16), dense_bf, dense_bf, dense, dense],
        scratch_shapes=[pltpu.VMEM((D_MODEL, ROUTE_TILE), F32),
                        pltpu.VMEM((2 * ROUTE_HEADS, PEER_N_KEYS, ROUTE_TILE), F32),
                        pltpu.VMEM((2 * ROUTE_HEADS, PEER_TOPK, ROUTE_TILE), F32),
                        pltpu.VMEM((2 * ROUTE_HEADS, PEER_TOPK, ROUTE_TILE), F32),
                        pltpu.VMEM((ROUTE_HEADS, CAND_TOTAL, ROUTE_TILE), F32)],
        compiler_params=_params("parallel"),
        name="peer_route",
    )(h, g, wq_t_hi, wq_t_lo, keys, cpos, cneg)


def _peer_dense_kernel(xt_ref, u_ref, v_ref, rank2_in, e2_in, nsel_ref, c1_ref, h_ref,
                       o_ref, act_scr, g_scr, w_scr, acc_scr, rank2_ref, e2_ref):
    j = pl.program_id(1)

    @pl.when(j == 0)
    def _():
        acc_scr[...] = jnp.zeros(acc_scr.shape, F32)
        rank2_ref[...] = rank2_in[...]
        e2_ref[...] = e2_in[...]

    pack = BF16_ROWS
    zero = jnp.zeros((1, LANES), BF16)
    n_cols = xt_ref.shape[1] // LANES
    a_per_chunk = EXPERT_CHUNK // PEER_N_KEYS
    for chunk in range(EXPERT_TILE // EXPERT_CHUNK):
        experts = slice(chunk * EXPERT_CHUNK, (chunk + 1) * EXPERT_CHUNK)
        for a_local in range(chunk * a_per_chunk, (chunk + 1) * a_per_chunk):
            a_row = slice(a_local, a_local + 1)
            rows = slice(a_local * PEER_N_KEYS, (a_local + 1) * PEER_N_KEYS)
            for col in range(n_cols):
                cols = slice(col * LANES, (col + 1) * LANES)
                gate = None
                for h in range(PEER_HEADS):
                    n_row = jnp.broadcast_to(nsel_ref[h, a_row, cols], (pack, LANES)).astype(BF16)
                    c_row = jnp.broadcast_to(c1_ref[h, a_row, cols], (pack, LANES)).astype(BF16)
                    n_all = jnp.concatenate([n_row] * (PEER_N_KEYS // pack), axis=0)
                    c_all = jnp.concatenate([c_row] * (PEER_N_KEYS // pack), axis=0)
                    term = jnp.where(rank2_ref[h, :, cols] < n_all, e2_ref[h, :, cols] * c_all, zero)
                    gate = term if gate is None else gate + term
                g_scr[rows, cols] = gate
        act_scr[experts, :] = _dot(u_ref[experts, :], xt_ref[...])
        for a_local in range(chunk * a_per_chunk, (chunk + 1) * a_per_chunk):
            rows = slice(a_local * PEER_N_KEYS, (a_local + 1) * PEER_N_KEYS)
            for col in range(n_cols):
                cols = slice(col * LANES, (col + 1) * LANES)
                act = act_scr[rows, cols]
                gelu = 0.5 * act * (1.0 + lax.erf(act * (1.0 / math.sqrt(2.0))))
                w_scr[rows, cols] = gelu.astype(BF16) * g_scr[rows, cols]
        acc_scr[...] += _dot(v_ref[:, experts], w_scr[experts, :])

    @pl.when(j == pl.num_programs(1) - 1)
    def _():
        o_ref[...] = h_ref[...] + acc_scr[...].T


def peer_dense(h, xt, u_bf, v_bf, rank2, e2, nsel, c1):
    tp = h.shape[0]
    n_exp = u_bf.shape[0]
    dense_spec = pl.BlockSpec((PEER_HEADS, PEER_N_KEYS, TOKEN_TILE), lambda i, j: (0, 0, i))
    a_spec = pl.BlockSpec((PEER_HEADS, EXPERT_TILE // PEER_N_KEYS, TOKEN_TILE),
                          lambda i, j: (0, j, i))
    return pl.pallas_call(
        _peer_dense_kernel,
        grid=(tp // TOKEN_TILE, n_exp // EXPERT_TILE),
        in_specs=[pl.BlockSpec((D_MODEL, TOKEN_TILE), lambda i, j: (0, i)),
                  pl.BlockSpec((EXPERT_TILE, D_MODEL), lambda i, j: (j, 0)),
                  pl.BlockSpec((D_MODEL, EXPERT_TILE), lambda i, j: (0, j)),
                  dense_spec, dense_spec, a_spec, a_spec,
                  pl.BlockSpec((TOKEN_TILE, D_MODEL), lambda i, j: (i, 0))],
        out_specs=pl.BlockSpec((TOKEN_TILE, D_MODEL), lambda i, j: (i, 0)),
        out_shape=jax.ShapeDtypeStruct((tp, D_MODEL), F32),
        scratch_shapes=[pltpu.VMEM((EXPERT_TILE, TOKEN_TILE), F32),
                        pltpu.VMEM((EXPERT_TILE, TOKEN_TILE), BF16),
                        pltpu.VMEM((EXPERT_TILE, TOKEN_TILE), BF16),
                        pltpu.VMEM((D_MODEL, TOKEN_TILE), F32),
                        pltpu.VMEM((PEER_HEADS, PEER_N_KEYS, TOKEN_TILE), BF16),
                        pltpu.VMEM((PEER_HEADS, PEER_N_KEYS, TOKEN_TILE), BF16)],
        compiler_params=_params("parallel", "arbitrary"),
        name="peer_dense",
    )(xt, u_bf, v_bf, rank2, e2, nsel, c1, h)


def peer_layer(h, g, wq, sub_keys, u, v):
    wq_t_hi, wq_t_lo = _split(wq.T)
    keys = sub_keys.reshape(2 * PEER_HEADS, PEER_N_KEYS, PEER_KEY_HALF)
    xt, rank2, e2, nsel, c1 = peer_route(h, g, wq_t_hi, wq_t_lo, keys)
    return peer_dense(h, xt, u.astype(BF16), v.astype(BF16).T, rank2, e2, nsel, c1)


def _bucket_starts():
    max_exact = T5_BUCKETS // 2
    d = np.arange(0, 4 * T5_MAX_DIST)
    df = np.maximum(d, 1).astype(np.float64)
    large = max_exact + (np.log(df / max_exact) / math.log(T5_MAX_DIST / max_exact)
                         * (T5_BUCKETS - max_exact)).astype(np.int64)
    bucket = np.where(d < max_exact, d, np.minimum(large, T5_BUCKETS - 1))
    starts = []
    for b in range(T5_BUCKETS):
        idx = np.nonzero(bucket == b)[0]
        if idx.size:
            starts.append((int(idx[0]), b))
    return starts


BUCKET_STARTS = _bucket_starts()


def _bias_from_dist(dist, table):
    out = None
    for start, b in BUCKET_STARTS:
        val = table(b)
        out = val + jnp.zeros(dist.shape, F32) if out is None else jnp.where(dist >= start, val, out)
    return out


def _moba_prompt_kernel(rb_ref, q_ref, k_ref, v_ref, o_ref,
                        kb_scr, vt_scr, km_scr, bias_scr, sel_scr, m_scr, l_scr, acc_scr):
    pair = pl.program_id(1)
    qt = pl.program_id(2)
    n_blk = k_ref.shape[0] // MOBA_BLOCK
    blk = MOBA_BLOCK
    lane = lax.broadcasted_iota(jnp.int32, (1, 2 * HEAD_DIM), 1)

    @pl.when(qt == 0)
    def _():
        for b in range(n_blk):
            kt = k_ref[b * blk:(b + 1) * blk, :]
            kb_scr[b] = kt.astype(BF16)
            km_scr[b:b + 1, :] = jnp.mean(kt, axis=0, keepdims=True)
            vt_scr[b] = v_ref[b * blk:(b + 1) * blk, :].T.astype(BF16)
        ik = lax.broadcasted_iota(jnp.int32, (blk, blk), 0)
        iq = lax.broadcasted_iota(jnp.int32, (blk, blk), 1)
        for hh in range(2):
            head = 2 * pair + hh
            for delta in range(2):
                dist = iq - ik + delta * blk
                bias = _bias_from_dist(jnp.maximum(dist, 0), lambda b: rb_ref[b, head])
                bias_scr[hh, delta] = jnp.where(dist >= 0, bias, NEG_INF)

    own = qt
    blk_iota = lax.broadcasted_iota(jnp.int32, (n_blk, blk), 0)
    heads = range(2)
    qh_bf, far_bias = [], []
    for hh in heads:
        head = 2 * pair + hh
        in_head = (lane >= hh * HEAD_DIM) & (lane < (hh + 1) * HEAD_DIM)
        qh = jnp.where(in_head, q_ref[...] * (HEAD_DIM ** -0.5), 0.0)
        qh_bf.append(qh.astype(BF16))
        km = jnp.where(in_head, km_scr[...], 0.0)
        gate = lax.dot_general(km, qh, (((1,), (1,)), ((), ())),
                               precision=lax.Precision.HIGHEST,
                               preferred_element_type=F32)
        gate = jnp.where(blk_iota < own, gate, NEG_INF)
        rank = jnp.zeros((n_blk, blk), F32)
        for b in range(n_blk):
            gb = gate[b:b + 1, :]
            ahead = (gb > gate) | ((gb == gate) & (b < blk_iota))
            rank = rank + jnp.where(ahead, 1.0, 0.0)
        sel_scr[hh] = jnp.where((blk_iota < own) & (rank < float(MOBA_TOPK)), 1.0, 0.0)
        far_bias.append(rb_ref[T5_BUCKETS - 1, head])

    def step(hh, blocks, first):
        v_rows = slice(hh * HEAD_DIM, (hh + 1) * HEAD_DIM)
        scores = []
        for kb, bias in blocks:
            s = _dot_nt(kb_scr[kb], qh_bf[hh])
            if first:
                s = s + bias
            else:
                s = s + (far_bias[hh] if bias is None else bias)
                s = jnp.where(sel_scr[hh, pl.ds(kb, 1), :] > 0.0, s, NEG_INF)
            scores.append(s)
        m_new = None if first else m_scr[hh]
        for s in scores:
            m_blk = jnp.max(s, axis=0, keepdims=True)
            m_new = m_blk if m_new is None else jnp.maximum(m_new, m_blk)
        pv, p_sum = None, None
        for (kb, _), s in zip(blocks, scores):
            p = jnp.exp(s - m_new)
            part = _dot(vt_scr[kb, v_rows, :], p.astype(BF16))
            part_sum = jnp.sum(p, axis=0, keepdims=True)
            pv = part if pv is None else pv + part
            p_sum = part_sum if p_sum is None else p_sum + part_sum
        if first:
            l_scr[hh] = p_sum
            acc_scr[v_rows, :] = pv
        else:
            alpha = jnp.exp(m_scr[hh] - m_new)
            l_scr[hh] = alpha * l_scr[hh] + p_sum
            acc_scr[v_rows, :] = alpha * acc_scr[v_rows, :] + pv
        m_scr[hh] = m_new

    for hh in heads:
        step(hh, [(own, bias_scr[hh, 0])], True)

    @pl.when(own >= 1)
    def _():
        for hh in heads:
            step(hh, [(own - 1, bias_scr[hh, 1])], False)

    n_far = jnp.maximum(own - 1, 0)

    def far_pair(i, carry):
        for hh in heads:
            step(hh, [(2 * i, None), (2 * i + 1, None)], False)
        return carry

    lax.fori_loop(0, n_far // 2, far_pair, 0)

    @pl.when(n_far % 2 == 1)
    def _():
        for hh in heads:
            step(hh, [(n_far - 1, None)], False)
    for hh in heads:
        v_rows = slice(hh * HEAD_DIM, (hh + 1) * HEAD_DIM)
        acc_scr[v_rows, :] = acc_scr[v_rows, :] / l_scr[hh]
    o_ref[...] = acc_scr[...].T


def moba_prompt(q, k, v, rel_bias, n_b, seq):
    assert seq % MOBA_BLOCK == 0
    nq = seq // MOBA_BLOCK
    width = 2 * HEAD_DIM
    grid_spec = pltpu.PrefetchScalarGridSpec(
        num_scalar_prefetch=0,
        grid=(n_b, N_HEADS // 2, nq),
        in_specs=[pl.BlockSpec(memory_space=pltpu.SMEM),
                  pl.BlockSpec((MOBA_BLOCK, width), lambda n, p, t: (n * nq + t, p)),
                  pl.BlockSpec((seq, width), lambda n, p, t: (n, p)),
                  pl.BlockSpec((seq, width), lambda n, p, t: (n, p))],
        out_specs=pl.BlockSpec((MOBA_BLOCK, width), lambda n, p, t: (n * nq + t, p)),
        scratch_shapes=[pltpu.VMEM((nq, MOBA_BLOCK, width), BF16),
                        pltpu.VMEM((nq, width, MOBA_BLOCK), BF16),
                        pltpu.VMEM((nq, width), F32),
                        pltpu.VMEM((2, 2, MOBA_BLOCK, MOBA_BLOCK), F32),
                        pltpu.VMEM((2, nq, MOBA_BLOCK), F32),
                        pltpu.VMEM((2, 1, MOBA_BLOCK), F32),
                        pltpu.VMEM((2, 1, MOBA_BLOCK), F32),
                        pltpu.VMEM((width, MOBA_BLOCK), F32)],
    )
    return pl.pallas_call(
        _moba_prompt_kernel,
        grid_spec=grid_spec,
        out_shape=jax.ShapeDtypeStruct((n_b * seq, D_MODEL), F32),
        compiler_params=_params("parallel", "parallel", "arbitrary"),
        name="moba_prompt",
    )(rel_bias, q, k, v)


def _moba_sample_kernel(pt_ref, rbt_ref, q_ref, kn_ref, vn_ref, *refs, n_pages):
    k_refs, v_refs, o_ref = refs[:n_pages], refs[n_pages:2 * n_pages], refs[2 * n_pages]
    blk = MOBA_BLOCK
    pages_per_blk = blk // PAGE_SIZE
    n_blk = n_pages // pages_per_blk
    stat = (N_HEADS, STAT_LANES)
    head_of_lane = lax.broadcasted_iota(jnp.int32, (N_HEADS, D_MODEL), 1) // HEAD_DIM
    head_of_row = lax.broadcasted_iota(jnp.int32, (N_HEADS, D_MODEL), 0)
    diag = head_of_lane == head_of_row
    qbd = jnp.where(diag, q_ref[0] * (HEAD_DIM ** -0.5), 0.0)
    q_hi, q_lo = _split(qbd)
    q_hilo = jnp.concatenate([q_hi, q_lo], axis=0)
    key_iota = lax.broadcasted_iota(jnp.int32, (N_HEADS, blk), 1)

    gates, maxes, sums, outs = [], [], [], []
    for b in range(n_blk):
        pages = range(pages_per_blk * b, pages_per_blk * (b + 1))
        raw = [_dot(q_hilo, k_refs[i][0].astype(BF16)) for i in pages]
        s = jnp.concatenate([r[:N_HEADS] + r[N_HEADS:] for r in raw], axis=1)
        gate = jnp.mean(s, axis=1, keepdims=True)
        dist = (n_blk - b) * blk - key_iota
        s = s + _bias_from_dist(dist, lambda t: rbt_ref[:, t:t + 1])
        m = jnp.max(s, axis=1, keepdims=True)
        p = jnp.exp(s - m).astype(BF16)
        out = None
        for idx, i in enumerate(pages):
            part = _dot_nt(p[:, idx * PAGE_SIZE:(idx + 1) * PAGE_SIZE], v_refs[i][0].astype(BF16))
            out = part if out is None else out + part
        gates.append(jnp.broadcast_to(gate, stat))
        maxes.append(jnp.broadcast_to(m, stat))
        sums.append(jnp.broadcast_to(jnp.sum(p.astype(F32), axis=1, keepdims=True), stat))
        outs.append(out)

    s_self = jnp.sum(qbd * kn_ref[0], axis=1, keepdims=True) + rbt_ref[:, 0:1]
    m_tot = jnp.broadcast_to(s_self, stat)
    keep = []
    for i in range(n_blk):
        rank = jnp.zeros(stat, F32)
        for i2 in range(n_blk):
            if i2 == i:
                continue
            ahead = (gates[i2] > gates[i]) if i2 > i else (gates[i2] >= gates[i])
            rank = rank + jnp.where(ahead, 1.0, 0.0)
        keep.append(rank < float(MOBA_TOPK))
        m_tot = jnp.where(keep[i], jnp.maximum(m_tot, maxes[i]), m_tot)
    w_self = jnp.exp(s_self - m_tot)
    l_tot = w_self
    out = w_self[:, 0:1] * vn_ref[0]
    for i in range(n_blk):
        w_i = jnp.where(keep[i], jnp.exp(maxes[i] - m_tot), 0.0)
        l_tot = l_tot + w_i * sums[i]
        out = out + w_i[:, 0:1] * outs[i]
    out = jnp.where(diag, out / l_tot[:, 0:1], 0.0)
    o_ref[0] = jnp.sum(out, axis=0, keepdims=True)


def moba_sample(q, k_new, v_new, cache_k, cache_v, page_table, rel_bias):
    n_d, n_pages = page_table.shape
    assert n_pages % (MOBA_BLOCK // PAGE_SIZE) == 0
    n_phys = cache_k.shape[0]
    ck = cache_k.transpose(0, 2, 3, 1).reshape(n_phys, D_MODEL, PAGE_SIZE)
    cv = cache_v.transpose(0, 2, 3, 1).reshape(n_phys, D_MODEL, PAGE_SIZE)
    row = pl.BlockSpec((1, 1, D_MODEL), lambda n, pt: (n, 0, 0))

    def page(i):
        return pl.BlockSpec((1, D_MODEL, PAGE_SIZE), lambda n, pt: (pt[n, i], 0, 0))

    pages = [page(i) for i in range(n_pages)]
    grid_spec = pltpu.PrefetchScalarGridSpec(
        num_scalar_prefetch=1,
        grid=(n_d,),
        in_specs=[pl.BlockSpec((N_HEADS, T5_BUCKETS), lambda n, pt: (0, 0)),
                  row, row, row] + pages + pages,
        out_specs=row,
    )
    out = pl.pallas_call(
        functools.partial(_moba_sample_kernel, n_pages=n_pages),
        grid_spec=grid_spec,
        out_shape=jax.ShapeDtypeStruct((n_d, 1, D_MODEL), F32),
        compiler_params=_params("parallel"),
        name="moba_sample",
    )(page_table, rel_bias.T, q[:, None, :], k_new[:, None, :], v_new[:, None, :],
      *([ck] * n_pages), *([cv] * n_pages))
    return out[:, 0, :]


def kernel(x_prompt, x_sample, state_conv, cache_k, cache_v, page_table, p_prompt, p_sample, rel_bias, norm_mix_g, norm_ffn_g, norm_ple_g, norm_final_g, conv_w_in, conv_b_in, conv_dw_w, conv_dw_b, conv_ln_g, conv_ln_b, conv_w_out, conv_b_out, attn_w_qkv, attn_w_o, peer_w_q, peer_sub_keys, peer_u, peer_v, ple_w_proj, ple_w_gate):
    n_b, seq, d = x_prompt.shape
    n_d, dec_seq, _ = x_sample.shape
    assert d == D_MODEL and dec_seq == 1 and seq % TOKEN_TILE == 0
    depth = norm_mix_g.shape[0]
    n_p = n_b * seq
    tp = -(-(n_p + n_d) // TOKEN_TILE) * TOKEN_TILE
    pad = tp - n_p - n_d

    def pack(prompt_rows, sample_rows):
        tail = jnp.zeros((pad, prompt_rows.shape[1]), prompt_rows.dtype)
        return jnp.concatenate([prompt_rows, sample_rows.astype(prompt_rows.dtype), tail], axis=0)

    def row(vec):
        return vec.reshape(1, -1)

    h = pack(x_prompt.reshape(n_p, d), x_sample.reshape(n_d, d))
    conv_p, conv_s, kp, vp, ks, vs = [], [], [], [], [], []
    zero_bias = jnp.zeros((1, d), F32)
    for i in range(depth):
        j = i // 2
        if i % 2 == 0:
            glu = glu_in(h, row(norm_mix_g[i]), conv_w_in[j].astype(BF16), row(conv_b_in[j]))
            dwk = jnp.concatenate([conv_dw_w[j][::-1], jnp.zeros((1, d), F32)], axis=0)
            y_p, tail = conv_prompt(glu, n_b, seq, dwk, row(conv_dw_b[j]),
                                    row(conv_ln_g[j]), row(conv_ln_b[j]))
            glu_s = glu[n_p:n_p + n_d]
            y_s = conv_sample(glu_s, state_conv[j], conv_dw_w[j], row(conv_dw_b[j]),
                              row(conv_ln_g[j]), row(conv_ln_b[j]))
            conv_p.append(tail[:, HALO - (CONV_WIDTH - 1):, :])
            conv_s.append(jnp.concatenate([state_conv[j][:, 1:, :], glu_s[:, None, :]], axis=1))
            h = proj_residual(h, pack(y_p, y_s), conv_w_out[j].astype(BF16), row(conv_b_out[j]))
        else:
            wq_hi, wq_lo = _split(attn_w_qkv[j][:, :d])
            q, k, v = qkv_proj(h, row(norm_mix_g[i]), wq_hi, wq_lo, attn_w_qkv[j][:, d:].astype(BF16))
            o_p = moba_prompt(q, k, v, rel_bias, n_b, seq)
            o_s = moba_sample(q[n_p:n_p + n_d], k[n_p:n_p + n_d], v[n_p:n_p + n_d],
                              cache_k[j], cache_v[j], page_table, rel_bias)
            kp.append(k[:n_p].reshape(n_b, seq, N_HEADS, HEAD_DIM))
            vp.append(v[:n_p].reshape(n_b, seq, N_HEADS, HEAD_DIM))
            ks.append(k[n_p:n_p + n_d].reshape(n_d, 1, N_HEADS, HEAD_DIM))
            vs.append(v[n_p:n_p + n_d].reshape(n_d, 1, N_HEADS, HEAD_DIM))
            h = proj_residual(h, pack(o_p, o_s), attn_w_o[j].astype(BF16), zero_bias)
        h = peer_layer(h, row(norm_ffn_g[i]), peer_w_q[i], peer_sub_keys[i], peer_u[i], peer_v[i])
        p_all = pack(p_prompt[i].reshape(n_p, -1), p_sample[i].reshape(n_d, -1))
        h = ple(h, p_all, row(norm_ple_g[i]), ple_w_gate[i].astype(BF16),
                ple_w_proj[i].astype(BF16), row(norm_final_g), final_norm=(i == depth - 1))
    y_prompt = h[:n_p].reshape(n_b, seq, d)
    y_sample = h[n_p:n_p + n_d].reshape(n_d, 1, d)
    return (y_prompt, y_sample, jnp.stack(conv_p), jnp.stack(conv_s),
            jnp.stack(kp), jnp.stack(vp), jnp.stack(ks), jnp.stack(vs))
```
